```python
import math
import jax, jax.numpy as jnp
from jax import lax
import numpy as np

D_MODEL = 2048
BATCH = 8
SEQ = 2048
DEPTH = 4
DEC_BATCH = 4
DEC_SEQ = 4096
PAST_LEN = 128

GRID_W = 64
HEAD_DIM = 128
N_MIXERS = 2
NA_HEADS = D_MODEL // HEAD_DIM
NA_ROWS = 8
NA_COLS = 16
DIL_CONFIG = ((128, 1), (512, 4), (2048, 16))
N_GROUPS = len(DIL_CONFIG)
DIL_HEADS = D_MODEL // (2 * HEAD_DIM)
N_BUCKETS = 32
MAX_DISTANCE = 1024
D_FF = 5632
N_NA_LAYERS = (DEPTH + 1) // 2
N_DIL_LAYERS = DEPTH // 2
ALPHA = (2 * DEPTH) ** 0.25
BETA = (8 * DEPTH) ** -0.25
LN_EPS = 1e-5
NEG_INF = -1e30

kernel_name = "hybrid_na_dilated_macaron_encoder"


def layer_norm(x, g, b):
    xf = x.astype(jnp.float32)
    mu = jnp.mean(xf, axis=-1, keepdims=True)
    var = jnp.mean(jnp.square(xf - mu), axis=-1, keepdims=True)
    y = (xf - mu) * lax.rsqrt(var + LN_EPS) * g.astype(jnp.float32) + b.astype(jnp.float32)
    return y.astype(x.dtype)


def post_norm(x, sub, g, b):
    return layer_norm(ALPHA * x + sub, g, b)


def swiglu(x, w_gate, w_up, w_down):
    return (jax.nn.silu(x @ w_gate) * (x @ w_up)) @ w_down


def t5_bucket(rel):
    half = N_BUCKETS // 2
    max_exact = half // 2
    sign = jnp.where(rel > 0, half, 0)
    n = jnp.abs(rel)
    nf = jnp.maximum(n, 1).astype(jnp.float32)
    large = max_exact + (jnp.log(nf / max_exact) / math.log(MAX_DISTANCE / max_exact)
                         * (half - max_exact)).astype(jnp.int32)
    large = jnp.minimum(large, half - 1)
    return sign + jnp.where(n < max_exact, n, large)


def neighbourhood_mixer(x, w_qkv, w_o, rpb):
    B, T, _ = x.shape
    rows = T // GRID_W
    kh = min(NA_ROWS, rows)
    qkv = (x @ w_qkv).reshape(B, rows, GRID_W, 3, NA_HEADS, HEAD_DIM)
    q, k, v = qkv[:, :, :, 0], qkv[:, :, :, 1], qkv[:, :, :, 2]
    r = jnp.arange(rows)
    row_idx = jnp.clip(r - kh // 2, 0, rows - kh)[:, None] + jnp.arange(kh)[None, :]
    k_rows = k[:, row_idx]
    v_rows = v[:, row_idx].reshape(B, rows, kh * GRID_W, NA_HEADS, HEAD_DIM)
    c = jnp.arange(GRID_W)
    col_start = jnp.clip(c - NA_COLS // 2, 0, GRID_W - NA_COLS)
    col_ok = (c[None, :] >= col_start[:, None]) & (c[None, :] < col_start[:, None] + NA_COLS)
    dr = row_idx - r[:, None] + (NA_ROWS - 1)
    dc = jnp.clip(c[None, :] - c[:, None], -(NA_COLS - 1), NA_COLS - 1) + (NA_COLS - 1)
    bias = rpb[:, dr[:, None, :, None], dc[None, :, None, :]].astype(jnp.float32)
    s = jnp.einsum('brqhd,brawhd->bhrqaw', q, k_rows).astype(jnp.float32) * HEAD_DIM ** -0.5
    s = jnp.where(col_ok[:, None, :], s + bias[None], NEG_INF)
    s = s.reshape(B, NA_HEADS, rows, GRID_W, kh * GRID_W)
    p = jax.nn.softmax(s, axis=-1).astype(v.dtype)
    o = jnp.einsum('bhrqk,brkhd->brqhd', p, v_rows)
    return o.reshape(B, T, NA_HEADS * HEAD_DIM) @ w_o


def banded_attention(q, k, v, radius, bias):
    n, L, H, dh = q.shape
    qb = radius
    kw = qb + 2 * radius
    nb = -(-L // qb)
    lp = nb * qb
    qblk = jnp.pad(q, ((0, 0), (0, lp - L), (0, 0), (0, 0))).reshape(n, nb, qb, H, dh)
    kv_pad = ((0, 0), (radius, lp - L + radius), (0, 0), (0, 0))
    idx = jnp.arange(nb)[:, None] * qb + jnp.arange(kw)[None, :]
    kblk = jnp.pad(k, kv_pad)[:, idx]
    vblk = jnp.pad(v, kv_pad)[:, idx]
    s = jnp.einsum('nbqhd,nbkhd->nhbqk', qblk, kblk).astype(jnp.float32) * dh ** -0.5
    s = s + bias[None, :, None]
    rel = jnp.arange(kw)[None, :] - radius - jnp.arange(qb)[:, None]
    kpos = idx - radius
    valid = ((kpos >= 0) & (kpos < L))[:, None, :] & (jnp.abs(rel) <= radius)[None]
    s = jnp.where(valid[None, None], s, NEG_INF)
    m = jnp.max(s, axis=-1, keepdims=True)
    p = jnp.exp(s - m)
    den = jnp.sum(p, axis=-1, keepdims=True)
    o = jnp.einsum('nhbqk,nbkhd->nbqhd', (p / den).astype(v.dtype), vblk)
    o = o.reshape(n, lp, H, dh)[:, :L]
    lse = (m + jnp.log(den))[..., 0]
    lse = lse.transpose(0, 2, 3, 1).reshape(n, lp, H)[:, :L]
    return o, lse


def dilated_attention(q, k, v, dil, radius, bias):
    B, T, H, dh = q.shape
    L = T // dil

    def to_sub(t):
        return t.reshape(B, L, dil, H, dh).transpose(0, 2, 1, 3, 4).reshape(B * dil, L, H, dh)

    o, lse = banded_attention(to_sub(q), to_sub(k), to_sub(v), radius, bias)
    o = o.reshape(B, dil, L, H, dh).transpose(0, 2, 1, 3, 4).reshape(B, T, H, dh)
    lse = lse.reshape(B, dil, L, H).transpose(0, 2, 1, 3).reshape(B, T, H)
    return o, lse


def dilated_mixer(x, w_qkv, w_o, rel_bias):
    B, T, _ = x.shape
    qkv = (x @ w_qkv).reshape(B, T, N_GROUPS, 3, DIL_HEADS, HEAD_DIM)
    outs, lses = [], []
    for g, (window, dil) in enumerate(DIL_CONFIG):
        radius = window // (2 * dil)
        rel = jnp.arange(3 * radius)[None, :] - radius - jnp.arange(radius)[:, None]
        bias = rel_bias[t5_bucket(rel * dil)][:, :, g * DIL_HEADS:(g + 1) * DIL_HEADS]
        bias = bias.transpose(2, 0, 1).astype(jnp.float32)
        o, lse = dilated_attention(qkv[:, :, g, 0], qkv[:, :, g, 1], qkv[:, :, g, 2], dil, radius, bias)
        outs.append(o)
        lses.append(lse)
    wts = jax.nn.softmax(jnp.stack(lses, axis=0), axis=0)
    o = jnp.sum(wts[..., None].astype(x.dtype) * jnp.stack(outs, axis=0), axis=0)
    return o.reshape(B, T, DIL_HEADS * HEAD_DIM) @ w_o


def trunk(x, ln_g, ln_b, ffn_w_gate, ffn_w_up, ffn_w_down, na_w_qkv, na_w_o, na_rpb,
          dil_w_qkv, dil_w_o, rel_bias):
    for i in range(DEPTH):
        j = i // N_MIXERS
        x = post_norm(x, 0.5 * swiglu(x, ffn_w_gate[i, 0], ffn_w_up[i, 0], ffn_w_down[i, 0]),
                      ln_g[i, 0], ln_b[i, 0])
        if i % N_MIXERS == 0:
            mix = neighbourhood_mixer(x, na_w_qkv[j], na_w_o[j], na_rpb[j])
        else:
            mix = dilated_mixer(x, dil_w_qkv[j], dil_w_o[j], rel_bias)
        x = post_norm(x, mix, ln_g[i, 1], ln_b[i, 1])
        x = post_norm(x, 0.5 * swiglu(x, ffn_w_gate[i, 1], ffn_w_up[i, 1], ffn_w_down[i, 1]),
                      ln_g[i, 2], ln_b[i, 2])
    return x


def setup_inputs(seed: int = 0) -> dict:
    key = jax.random.key(seed)
    ks = jax.random.split(key, 14)

    def nrm(k, shape, scale):
        return jax.random.normal(k, shape, jnp.float32) * scale

    na_width = NA_HEADS * HEAD_DIM
    dil_width = DIL_HEADS * HEAD_DIM
    return {
        "x_prompt": nrm(ks[0], (BATCH, SEQ, D_MODEL), 1.0),
        "x_sample": nrm(ks[1], (DEC_BATCH, DEC_SEQ, D_MODEL), 1.0),
        "ln_g": 1.0 + nrm(ks[2], (DEPTH, 3, D_MODEL), 0.02),
        "ln_b": nrm(ks[3], (DEPTH, 3, D_MODEL), 0.02),
        "ffn_w_gate": nrm(ks[4], (DEPTH, 2, D_MODEL, D_FF), D_MODEL ** -0.5),
        "ffn_w_up": nrm(ks[5], (DEPTH, 2, D_MODEL, D_FF), D_MODEL ** -0.5),
        "ffn_w_down": nrm(ks[6], (DEPTH, 2, D_FF, D_MODEL), BETA * D_FF ** -0.5),
        "na_w_qkv": nrm(ks[7], (N_NA_LAYERS, D_MODEL, 3 * na_width), D_MODEL ** -0.5),
        "na_w_o": nrm(ks[8], (N_NA_LAYERS, na_width, D_MODEL), BETA * na_width ** -0.5),
        "na_rpb": nrm(ks[9], (N_NA_LAYERS, NA_HEADS, 2 * NA_ROWS - 1, 2 * NA_COLS - 1), 0.2),
        "dil_w_qkv": nrm(ks[10], (N_DIL_LAYERS, D_MODEL, N_GROUPS * 3 * dil_width), D_MODEL ** -0.5),
        "dil_w_o": nrm(ks[11], (N_DIL_LAYERS, dil_width, D_MODEL), BETA * dil_width ** -0.5),
        "rel_bias": nrm(ks[12], (N_BUCKETS, N_GROUPS * DIL_HEADS), 0.2),
    }


def reference(x_prompt, x_sample, ln_g, ln_b, ffn_w_gate, ffn_w_up, ffn_w_down, na_w_qkv, na_w_o,
              na_rpb, dil_w_qkv, dil_w_o, rel_bias):
    y_prompt = trunk(x_prompt, ln_g, ln_b, ffn_w_gate, ffn_w_up, ffn_w_down, na_w_qkv, na_w_o,
                     na_rpb, dil_w_qkv, dil_w_o, rel_bias)
    y_sample = trunk(x_sample, ln_g, ln_b, ffn_w_gate, ffn_w_up, ffn_w_down, na_w_qkv, na_w_o,
                     na_rpb, dil_w_qkv, dil_w_o, rel_bias)
    return (y_prompt, y_sample)
```

```python
import functools
import math

import jax
import jax.numpy as jnp
from jax import lax
from jax.experimental import pallas as pl
from jax.experimental.pallas import tpu as pltpu

D_MODEL = 2048
DEPTH = 4
GRID_W = 64
HEAD_DIM = 128
N_MIXERS = 2
NA_HEADS = D_MODEL // HEAD_DIM
NA_ROWS = 8
NA_COLS = 16
DIL_CONFIG = ((128, 1), (512, 4), (2048, 16))
N_GROUPS = len(DIL_CONFIG)
DIL_HEADS = D_MODEL // (2 * HEAD_DIM)
DIL_RADIUS = 64
N_BUCKETS = 32
MAX_DISTANCE = 1024
D_FF = 5632
ALPHA = (2 * DEPTH) ** 0.25
LN_EPS = 1e-5
NEG_INF = -1e30
SCALE = HEAD_DIM ** -0.5

VMEM_LIMIT_BYTES = 56 * 1024 * 1024

BF16 = jnp.bfloat16
F32 = jnp.float32

DIL_QB = 128
DIL_KB = DIL_QB + 2 * DIL_RADIUS


def _layer_norm(y, g, b):
    mu = jnp.mean(y, axis=-1, keepdims=True)
    yc = y - mu
    var = jnp.mean(yc * yc, axis=-1, keepdims=True)
    return yc * lax.rsqrt(var + LN_EPS) * g + b


def _params(*semantics):
    return pltpu.CompilerParams(dimension_semantics=semantics, vmem_limit_bytes=VMEM_LIMIT_BYTES)


def _ffn_kernel(x_ref, wg_ref, wu_ref, wd_ref, g_ref, b_ref, o_ref, xb_ref):
    k = pl.program_id(1)

    @pl.when(k == 0)
    def _init():
        xb_ref[...] = x_ref[...].astype(BF16)
        o_ref[...] = jnp.zeros_like(o_ref)

    xb = xb_ref[...]
    gate = jnp.dot(xb, wg_ref[...], preferred_element_type=F32)
    up = jnp.dot(xb, wu_ref[...], preferred_element_type=F32)
    h = (gate * (1.0 / (1.0 + jnp.exp(-gate))) * up).astype(BF16)
    o_ref[...] += jnp.dot(h, wd_ref[...], preferred_element_type=F32)

    @pl.when(k == pl.num_programs(1) - 1)
    def _finish():
        y = ALPHA * x_ref[...] + 0.5 * o_ref[...]
        o_ref[...] = _layer_norm(y, g_ref[...], b_ref[...])


def _ffn(x, wg, wu, wd, g, b, *, tm=512, tf=512):
    m = x.shape[0]
    assert m % tm == 0 and D_FF % tf == 0
    return pl.pallas_call(
        _ffn_kernel,
        out_shape=jax.ShapeDtypeStruct((m, D_MODEL), F32),
        grid=(m // tm, D_FF // tf),
        in_specs=[
            pl.BlockSpec((tm, D_MODEL), lambda i, k: (i, 0)),
            pl.BlockSpec((D_MODEL, tf), lambda i, k: (0, k)),
            pl.BlockSpec((D_MODEL, tf), lambda i, k: (0, k)),
            pl.BlockSpec((tf, D_MODEL), lambda i, k: (k, 0)),
            pl.BlockSpec((1, D_MODEL), lambda i, k: (0, 0)),
            pl.BlockSpec((1, D_MODEL), lambda i, k: (0, 0)),
        ],
        out_specs=pl.BlockSpec((tm, D_MODEL), lambda i, k: (i, 0)),
        scratch_shapes=[pltpu.VMEM((tm, D_MODEL), BF16)],
        compiler_params=_params("parallel", "arbitrary"),
        name="ffn",
    )(x, wg, wu, wd, g, b)


def _proj_kernel(x_ref, w_ref, o_ref, xb_ref):
    @pl.when(pl.program_id(1) == 0)
    def _init():
        xb_ref[...] = x_ref[...].astype(BF16)

    o_ref[...] = jnp.dot(xb_ref[...], w_ref[...], preferred_element_type=F32).astype(o_ref.dtype)


def _proj(x, w, *, tm=512, tn=1024):
    m, n = x.shape[0], w.shape[1]
    assert m % tm == 0 and n % tn == 0
    return pl.pallas_call(
        _proj_kernel,
        out_shape=jax.ShapeDtypeStruct((m, n), BF16),
        grid=(m // tm, n // tn),
        in_specs=[
            pl.BlockSpec((tm, D_MODEL), lambda i, j: (i, 0)),
            pl.BlockSpec((D_MODEL, tn), lambda i, j: (0, j)),
        ],
        out_specs=pl.BlockSpec((tm, tn), lambda i, j: (i, j)),
        scratch_shapes=[pltpu.VMEM((tm, D_MODEL), BF16)],
        compiler_params=_params("parallel", "arbitrary"),
        name="qkv_proj",
    )(x, w)


def _oproj_kernel(a_ref, w_ref, x_ref, g_ref, b_ref, o_ref):
    mix = jnp.dot(a_ref[...], w_ref[...], preferred_element_type=F32)
    y = ALPHA * x_ref[...] + mix
    o_ref[...] = _layer_norm(y, g_ref[...], b_ref[...])


def _oproj(a, w, x, g, b, *, tm=512):
    m, kdim = a.shape
    assert m % tm == 0
    return pl.pallas_call(
        _oproj_kernel,
        out_shape=jax.ShapeDtypeStruct((m, D_MODEL), F32),
        grid=(m // tm,),
        in_specs=[
            pl.BlockSpec((tm, kdim), lambda i: (i, 0)),
            pl.BlockSpec((kdim, D_MODEL), lambda i: (0, 0)),
            pl.BlockSpec((tm, D_MODEL), lambda i: (i, 0)),
            pl.BlockSpec((1, D_MODEL), lambda i: (0, 0)),
            pl.BlockSpec((1, D_MODEL), lambda i: (0, 0)),
        ],
        out_specs=pl.BlockSpec((tm, D_MODEL), lambda i: (i, 0)),
        compiler_params=_params("parallel"),
        name="out_proj",
    )(a, w, x, g, b)


def _na_kernel(q_ref, k_ref, v_ref, bias_ref, o_ref, *, rows):
    kw = NA_ROWS * GRID_W

    def body(r, carry):
        rs = jnp.clip(r - NA_ROWS // 2, 0, rows - NA_ROWS)
        dr0 = rs - r + (NA_ROWS - 1)
        q = q_ref[pl.ds(pl.multiple_of(r * GRID_W, GRID_W), GRID_W), :]
        kstart = pl.multiple_of(rs * GRID_W, GRID_W)
        k = k_ref[pl.ds(kstart, kw), :]
        v = v_ref[pl.ds(kstart, kw), :]
        s = lax.dot_general(q, k, (((1,), (1,)), ((), ())), preferred_element_type=F32)
        s = s * SCALE + bias_ref[dr0]
        m = jnp.max(s, axis=-1, keepdims=True)
        p = jnp.exp(s - m)
        den = jnp.sum(p, axis=-1, keepdims=True)
        o = jnp.dot(p.astype(BF16), v, preferred_element_type=F32)
        o_ref[pl.ds(pl.multiple_of(r * GRID_W, GRID_W), GRID_W), :] = (o / den).astype(o_ref.dtype)
        return carry

    lax.fori_loop(0, rows, body, 0)


def _na_bias_slabs(rpb):
    c = jnp.arange(GRID_W)
    col_start = jnp.clip(c - NA_COLS // 2, 0, GRID_W - NA_COLS)
    col_ok = (c[None, :] >= col_start[:, None]) & (c[None, :] < col_start[:, None] + NA_COLS)
    dc = jnp.clip(c[None, :] - c[:, None], -(NA_COLS - 1), NA_COLS - 1) + (NA_COLS - 1)
    dr = jnp.arange(NA_ROWS)[:, None] + jnp.arange(NA_ROWS)[None, :]
    bias = rpb[:, dr[:, None, :, None], dc[None, :, None, :]].astype(F32)
    bias = jnp.where(col_ok[None, None, :, None, :], bias, NEG_INF)
    return bias.reshape(rpb.shape[0], NA_ROWS, GRID_W, NA_ROWS * GRID_W)


def _na_attention(qkv, slabs, *, batch, seq):
    rows = seq // GRID_W
    assert rows >= NA_ROWS and seq % GRID_W == 0
    blk = (None, seq, HEAD_DIM)
    return pl.pallas_call(
        functools.partial(_na_kernel, rows=rows),
        out_shape=jax.ShapeDtypeStruct((batch, seq, NA_HEADS * HEAD_DIM), BF16),
        grid=(batch, NA_HEADS),
        in_specs=[
            pl.BlockSpec(blk, lambda b, h: (b, 0, h)),
            pl.BlockSpec(blk, lambda b, h: (b, 0, NA_HEADS + h)),
            pl.BlockSpec(blk, lambda b, h: (b, 0, 2 * NA_HEADS + h)),
            pl.BlockSpec((None, NA_ROWS, GRID_W, NA_ROWS * GRID_W), lambda b, h: (h, 0, 0, 0)),
        ],
        out_specs=pl.BlockSpec(blk, lambda b, h: (b, 0, h)),
        compiler_params=_params("parallel", "parallel"),
        name="na_attention",
    )(qkv, qkv, qkv, slabs)


def _dil_kernel(*refs, seq):
    qkv_refs = refs[:9]
    bias_ref, o_ref = refs[9], refs[10]
    x32, qs, kp, vp, osub, lsub = refs[11:17]
    otok = refs[17:20]
    ltok = refs[20:23]

    for g, (_, d) in enumerate(DIL_CONFIG):
        q_ref, k_ref, v_ref = qkv_refs[3 * g:3 * g + 3]
        sub_len = seq // d
        nb = sub_len // DIL_QB
        seg = sub_len + 2 * DIL_RADIUS
        assert nb * DIL_QB == sub_len and nb & (nb - 1) == 0

        kp[...] = jnp.zeros_like(kp)
        vp[...] = jnp.zeros_like(vp)
        if d == 1:
            qs[...] = q_ref[...]
            kp[DIL_RADIUS:DIL_RADIUS + seq, :] = k_ref[...]
            vp[DIL_RADIUS:DIL_RADIUS + seq, :] = v_ref[...]
        else:
            x32[...] = q_ref[...].astype(F32)
            for res in range(d):
                qs[res * sub_len:(res + 1) * sub_len, :] = x32[pl.ds(res, sub_len, stride=d), :].astype(BF16)
            for src, dst in ((k_ref, kp), (v_ref, vp)):
                x32[...] = src[...].astype(F32)
                for res in range(d):
                    lo = res * seg + DIL_RADIUS
                    dst[lo:lo + sub_len, :] = x32[pl.ds(res, sub_len, stride=d), :].astype(BF16)

        def body(n, carry, g=g, nb=nb):
            res = lax.shift_right_logical(n, nb.bit_length() - 1)
            i = n - res * nb
            qoff = pl.multiple_of(n * DIL_QB, DIL_QB)
            koff = pl.multiple_of((n + res) * DIL_QB, DIL_QB)
            q = qs[pl.ds(qoff, DIL_QB), :]
            k = kp[pl.ds(koff, DIL_KB), :]
            v = vp[pl.ds(koff, DIL_KB), :]
            s = lax.dot_general(q, k, (((1,), (1,)), ((), ())), preferred_element_type=F32)
            s = s * SCALE + bias_ref[g]
            kj = lax.broadcasted_iota(jnp.int32, (DIL_QB, DIL_KB), 1)
            lo = jnp.where(i == 0, DIL_RADIUS, 0)
            hi = jnp.where(i == nb - 1, DIL_QB + DIL_RADIUS, DIL_KB)
            s = jnp.where((kj >= lo) & (kj < hi), s, NEG_INF)
            m = jnp.max(s, axis=-1, keepdims=True)
            p = jnp.exp(s - m)
            den = jnp.sum(p, axis=-1, keepdims=True)
            o = jnp.dot(p.astype(BF16), v, preferred_element_type=F32)
            osub[pl.ds(qoff, DIL_QB), :] = o / den
            lsub[pl.ds(qoff, DIL_QB), :] = jnp.broadcast_to(m + jnp.log(den), (DIL_QB, HEAD_DIM))
            return carry

        lax.fori_loop(0, seq // DIL_QB, body, 0)

        if d == 1:
            otok[g][...] = osub[...]
            ltok[g][...] = lsub[...]
        else:
            for res in range(d):
                otok[g][pl.ds(res, sub_len, stride=d), :] = osub[res * sub_len:(res + 1) * sub_len, :]
                ltok[g][pl.ds(res, sub_len, stride=d), :] = lsub[res * sub_len:(res + 1) * sub_len, :]

    chunk = 256

    def merge(c, carry):
        rows = pl.ds(pl.multiple_of(c * chunk, chunk), chunk)
        l0, l1, l2 = ltok[0][rows, :], ltok[1][rows, :], ltok[2][rows, :]
        m = jnp.maximum(jnp.maximum(l0, l1), l2)
        e0, e1, e2 = jnp.exp(l0 - m), jnp.exp(l1 - m), jnp.exp(l2 - m)
        num = e0 * otok[0][rows, :] + e1 * otok[1][rows, :] + e2 * otok[2][rows, :]
        o_ref[rows, :] = (num / (e0 + e1 + e2)).astype(o_ref.dtype)
        return carry

    lax.fori_loop(0, seq // chunk, merge, 0)


def _t5_bucket(rel):
    half = N_BUCKETS // 2
    max_exact = half // 2
    sign = jnp.where(rel > 0, half, 0)
    n = jnp.abs(rel)
    nf = jnp.maximum(n, 1).astype(F32)
    large = max_exact + (jnp.log(nf / max_exact) / math.log(MAX_DISTANCE / max_exact)
                         * (half - max_exact)).astype(jnp.int32)
    large = jnp.minimum(large, half - 1)
    return sign + jnp.where(n < max_exact, n, large)


def _dil_bias_tiles(rel_bias):
    rel = jnp.arange(DIL_KB)[None, :] - DIL_RADIUS - jnp.arange(DIL_QB)[:, None]
    in_band = jnp.abs(rel) <= DIL_RADIUS
    tiles = []
    for g, (window, d) in enumerate(DIL_CONFIG):
        assert window // (2 * d) == DIL_RADIUS
        bias = rel_bias[_t5_bucket(rel * d)][:, :, g * DIL_HEADS:(g + 1) * DIL_HEADS]
        bias = bias.transpose(2, 0, 1).astype(F32)
        tiles.append(jnp.where(in_band[None], bias, NEG_INF))
    return jnp.stack(tiles, axis=0)


def _dil_attention(qkv, tiles, *, batch, seq):
    max_d = max(d for _, d in DIL_CONFIG)
    assert seq % (max_d * DIL_QB) == 0
    blk = (None, seq, HEAD_DIM)

    def col(g, s):
        return lambda b, h: (b, 0, (3 * g + s) * DIL_HEADS + h)

    in_specs = [pl.BlockSpec(blk, col(g, s)) for g in range(N_GROUPS) for s in range(3)]
    in_specs.append(pl.BlockSpec((N_GROUPS, None, DIL_QB, DIL_KB), lambda b, h: (0, h, 0, 0)))
    pad_rows = seq + 2 * DIL_RADIUS * max_d
    scratch = [
        pltpu.VMEM((seq, HEAD_DIM), F32),
        pltpu.VMEM((seq, HEAD_DIM), BF16),
        pltpu.VMEM((pad_rows, HEAD_DIM), BF16),
        pltpu.VMEM((pad_rows, HEAD_DIM), BF16),
        pltpu.VMEM((seq, HEAD_DIM), F32),
        pltpu.VMEM((seq, HEAD_DIM), F32),
    ]
    scratch += [pltpu.VMEM((seq, HEAD_DIM), F32) for _ in range(2 * N_GROUPS)]
    return pl.pallas_call(
        functools.partial(_dil_kernel, seq=seq),
        out_shape=jax.ShapeDtypeStruct((batch, seq, DIL_HEADS * HEAD_DIM), BF16),
        grid=(batch, DIL_HEADS),
        in_specs=in_specs,
        out_specs=pl.BlockSpec(blk, lambda b, h: (b, 0, h)),
        scratch_shapes=scratch,
        compiler_params=_params("parallel", "parallel"),
        name="dil_attention",
    )(*([qkv] * 9), tiles)


def _trunk(x, p):
    batch, seq, _ = x.shape
    x = x.reshape(batch * seq, D_MODEL)
    for i in range(DEPTH):
        j = i // N_MIXERS
        x = _ffn(x, p["wg"][i, 0], p["wu"][i, 0], p["wd"][i, 0], p["ln_g"][i, 0], p["ln_b"][i, 0])
        if i % N_MIXERS == 0:
            qkv = _proj(x, p["na_qkv"][j]).reshape(batch, seq, -1)
            a = _na_attention(qkv, p["na_slabs"][j], batch=batch, seq=seq)
            w_o = p["na_o"][j]
        else:
            qkv = _proj(x, p["dil_qkv"][j]).reshape(batch, seq, -1)
            a = _dil_attention(qkv, p["dil_tiles"], batch=batch, seq=seq)
            w_o = p["dil_o"][j]
        x = _oproj(a.reshape(batch * seq, -1), w_o, x, p["ln_g"][i, 1], p["ln_b"][i, 1])
        x = _ffn(x, p["wg"][i, 1], p["wu"][i, 1], p["wd"][i, 1], p["ln_g"][i, 2], p["ln_b"][i, 2])
    return x.reshape(batch, seq, D_MODEL)


def kernel(x_prompt, x_sample, ln_g, ln_b, ffn_w_gate, ffn_w_up, ffn_w_down, na_w_qkv, na_w_o, na_rpb, dil_w_qkv, dil_w_o, rel_bias):
    p = {
        "ln_g": ln_g.astype(F32).reshape(DEPTH, 3, 1, D_MODEL),
        "ln_b": ln_b.astype(F32).reshape(DEPTH, 3, 1, D_MODEL),
        "wg": ffn_w_gate.astype(BF16),
        "wu": ffn_w_up.astype(BF16),
        "wd": ffn_w_down.astype(BF16),
        "na_qkv": na_w_qkv.astype(BF16),
        "na_o": na_w_o.astype(BF16),
        "na_slabs": jax.vmap(_na_bias_slabs)(na_rpb),
        "dil_qkv": dil_w_qkv.astype(BF16),
        "dil_o": dil_w_o.astype(BF16),
        "dil_tiles": _dil_bias_tiles(rel_bias),
    }
    return (_trunk(x_prompt, p), _trunk(x_sample, p))
```

```python
import functools
import math

import jax
import jax.numpy as jnp
import numpy as np
from jax import lax
from jax.experimental import pallas as pl
from jax.experimental.pallas import tpu as pltpu

D_MODEL = 2048
DEPTH = 4
GRID_W = 64
HEAD_DIM = 128
N_MIXERS = 2
NA_HEADS = D_MODEL // HEAD_DIM
NA_ROWS = 8
NA_COLS = 16
DIL_CONFIG = ((128, 1), (512, 4), (2048, 16))
N_GROUPS = len(DIL_CONFIG)
DIL_HEADS = D_MODEL // (2 * HEAD_DIM)
DIL_RADIUS = 64
N_BUCKETS = 32
MAX_DISTANCE = 1024
D_FF = 5632
ALPHA = (2 * DEPTH) ** 0.25
LN_EPS = 1e-5
NEG_INF = -1e30
SCALE = HEAD_DIM ** -0.5

VMEM_LIMIT_BYTES = 56 * 1024 * 1024

BF16 = jnp.bfloat16
F32 = jnp.float32

DIL_QB = 128
DIL_KB = DIL_QB + 2 * DIL_RADIUS

ATTN_UNROLL = 8


def _layer_norm(y, g, b):
    mu = jnp.mean(y, axis=-1, keepdims=True)
    yc = y - mu
    var = jnp.mean(yc * yc, axis=-1, keepdims=True)
    return yc * lax.rsqrt(var + LN_EPS) * g + b


def _params(*semantics):
    return pltpu.CompilerParams(dimension_semantics=semantics, vmem_limit_bytes=VMEM_LIMIT_BYTES)


def _ffn_kernel(x_ref, wg_ref, wu_ref, wd_ref, g_ref, b_ref, o_ref, xb_ref):
    k = pl.program_id(1)

    @pl.when(k == 0)
    def _init():
        xb_ref[...] = x_ref[...].astype(BF16)
        o_ref[...] = jnp.zeros_like(o_ref)

    xb = xb_ref[...]
    gate = jnp.dot(xb, wg_ref[...], preferred_element_type=F32)
    up = jnp.dot(xb, wu_ref[...], preferred_element_type=F32)
    h = (gate * (1.0 / (1.0 + jnp.exp(-gate))) * up).astype(BF16)
    o_ref[...] += jnp.dot(h, wd_ref[...], preferred_element_type=F32)

    @pl.when(k == pl.num_programs(1) - 1)
    def _finish():
        y = ALPHA * x_ref[...] + 0.5 * o_ref[...]
        o_ref[...] = _layer_norm(y, g_ref[...], b_ref[...])


def _ffn(x, wg, wu, wd, g, b, *, tm=512, tf=512):
    m = x.shape[0]
    assert m % tm == 0 and D_FF % tf == 0
    return pl.pallas_call(
        _ffn_kernel,
        out_shape=jax.ShapeDtypeStruct((m, D_MODEL), F32),
        grid=(m // tm, D_FF // tf),
        in_specs=[
            pl.BlockSpec((tm, D_MODEL), lambda i, k: (i, 0)),
            pl.BlockSpec((D_MODEL, tf), lambda i, k: (0, k)),
            pl.BlockSpec((D_MODEL, tf), lambda i, k: (0, k)),
            pl.BlockSpec((tf, D_MODEL), lambda i, k: (k, 0)),
            pl.BlockSpec((1, D_MODEL), lambda i, k: (0, 0)),
            pl.BlockSpec((1, D_MODEL), lambda i, k: (0, 0)),
        ],
        out_specs=pl.BlockSpec((tm, D_MODEL), lambda i, k: (i, 0)),
        scratch_shapes=[pltpu.VMEM((tm, D_MODEL), BF16)],
        compiler_params=_params("parallel", "arbitrary"),
        name="ffn",
    )(x, wg, wu, wd, g, b)


def _proj_kernel(x_ref, w_ref, o_ref, xb_ref):
    @pl.when(pl.program_id(1) == 0)
    def _init():
        xb_ref[...] = x_ref[...].astype(BF16)

    o_ref[...] = jnp.dot(xb_ref[...], w_ref[...], preferred_element_type=F32).astype(o_ref.dtype)


def _proj(x, w, *, tm=512, tn=1024):
    m, n = x.shape[0], w.shape[1]
    assert m % tm == 0 and n % tn == 0
    return pl.pallas_call(
        _proj_kernel,
        out_shape=jax.ShapeDtypeStruct((m, n), BF16),
        grid=(m // tm, n // tn),
        in_specs=[
            pl.BlockSpec((tm, D_MODEL), lambda i, j: (i, 0)),
            pl.BlockSpec((D_MODEL, tn), lambda i, j: (0, j)),
        ],
        out_specs=pl.BlockSpec((tm, tn), lambda i, j: (i, j)),
        scratch_shapes=[pltpu.VMEM((tm, D_MODEL), BF16)],
        compiler_params=_params("parallel", "arbitrary"),
        name="qkv_proj",
    )(x, w)


def _oproj_kernel(a_ref, w_ref, x_ref, g_ref, b_ref, o_ref):
    mix = jnp.dot(a_ref[...], w_ref[...], preferred_element_type=F32)
    y = ALPHA * x_ref[...] + mix
    o_ref[...] = _layer_norm(y, g_ref[...], b_ref[...])


def _oproj(a, w, x, g, b, *, tm=512):
    m, kdim = a.shape
    assert m % tm == 0
    return pl.pallas_call(
        _oproj_kernel,
        out_shape=jax.ShapeDtypeStruct((m, D_MODEL), F32),
        grid=(m // tm,),
        in_specs=[
            pl.BlockSpec((tm, kdim), lambda i: (i, 0)),
            pl.BlockSpec((kdim, D_MODEL), lambda i: (0, 0)),
            pl.BlockSpec((tm, D_MODEL), lambda i: (i, 0)),
            pl.BlockSpec((1, D_MODEL), lambda i: (0, 0)),
            pl.BlockSpec((1, D_MODEL), lambda i: (0, 0)),
        ],
        out_specs=pl.BlockSpec((tm, D_MODEL), lambda i: (i, 0)),
        compiler_params=_params("parallel"),
        name="out_proj",
    )(a, w, x, g, b)


def _na_kernel(q_ref, k_ref, v_ref, bias_ref, o_ref, *, rows):
    kw = NA_ROWS * GRID_W

    def body(it, carry):
        rr = [it * ATTN_UNROLL + u for u in range(ATTN_UNROLL)]
        rs = [jnp.clip(r - NA_ROWS // 2, 0, rows - NA_ROWS) for r in rr]
        kstart = [pl.multiple_of(x * GRID_W, GRID_W) for x in rs]
        qstart = [pl.multiple_of(r * GRID_W, GRID_W) for r in rr]
        s = [lax.dot_general(q_ref[pl.ds(qstart[u], GRID_W), :], k_ref[pl.ds(kstart[u], kw), :],
                             (((1,), (1,)), ((), ())), preferred_element_type=F32)
             for u in range(ATTN_UNROLL)]
        p, den = [], []
        for u in range(ATTN_UNROLL):
            su = s[u] * SCALE + bias_ref[rs[u] - rr[u] + (NA_ROWS - 1)]
            pu = jnp.exp(su - jnp.max(su, axis=-1, keepdims=True))
            den.append(jnp.sum(pu, axis=-1, keepdims=True))
            p.append(pu.astype(BF16))
        o = [jnp.dot(p[u], v_ref[pl.ds(kstart[u], kw), :], preferred_element_type=F32)
             for u in range(ATTN_UNROLL)]
        for u in range(ATTN_UNROLL):
            o_ref[pl.ds(qstart[u], GRID_W), :] = (o[u] / den[u]).astype(o_ref.dtype)
        return carry

    assert rows % ATTN_UNROLL == 0
    lax.fori_loop(0, rows // ATTN_UNROLL, body, 0)


def _toeplitz(f, n_rows, n_cols, offset):
    p = f.shape[-1]
    assert offset - (n_rows - 1) >= 0 and offset + n_cols <= p - 1
    lead = f.shape[:-1]
    x = jnp.broadcast_to(f[..., None, :], lead + (n_rows, p)).reshape(lead + (n_rows * p,))
    y = x[..., :n_rows * (p - 1)].reshape(lead + (n_rows, p - 1))
    return y[..., offset:offset + n_cols]


def _na_bias_slabs(rpb):
    h = rpb.shape[0]
    c = np.arange(GRID_W)
    col_start = np.clip(c - NA_COLS // 2, 0, GRID_W - NA_COLS)
    col_ok = (c[None, :] >= col_start[:, None]) & (c[None, :] < col_start[:, None] + NA_COLS)
    edge = GRID_W - NA_COLS
    f = jnp.concatenate([jnp.broadcast_to(rpb[..., :1], rpb.shape[:-1] + (edge,)), rpb,
                         jnp.broadcast_to(rpb[..., -1:], rpb.shape[:-1] + (edge + 1,))], axis=-1).astype(F32)
    table = jnp.where(col_ok, _toeplitz(f, GRID_W, GRID_W, GRID_W - 1), NEG_INF)
    slabs = jnp.stack([table[:, d0:d0 + NA_ROWS] for d0 in range(NA_ROWS)], axis=1)
    return slabs.transpose(0, 1, 3, 2, 4).reshape(h, NA_ROWS, GRID_W, NA_ROWS * GRID_W)


def _na_attention(qkv, slabs, *, batch, seq):
    rows = seq // GRID_W
    assert rows >= NA_ROWS and seq % GRID_W == 0
    blk = (None, seq, HEAD_DIM)
    return pl.pallas_call(
        functools.partial(_na_kernel, rows=rows),
        out_shape=jax.ShapeDtypeStruct((batch, seq, NA_HEADS * HEAD_DIM), BF16),
        grid=(batch, NA_HEADS),
        in_specs=[
            pl.BlockSpec(blk, lambda b, h: (b, 0, h)),
            pl.BlockSpec(blk, lambda b, h: (b, 0, NA_HEADS + h)),
            pl.BlockSpec(blk, lambda b, h: (b, 0, 2 * NA_HEADS + h)),
            pl.BlockSpec((None, NA_ROWS, GRID_W, NA_ROWS * GRID_W), lambda b, h: (h, 0, 0, 0)),
        ],
        out_specs=pl.BlockSpec(blk, lambda b, h: (b, 0, h)),
        compiler_params=_params("parallel", "parallel"),
        name="na_attention",
    )(qkv, qkv, qkv, slabs)


def _dil_kernel(*refs, seq):
    qkv_refs = refs[:9]
    bias_ref, o_ref = refs[9], refs[10]
    x32, qs, kp, vp, osub, lsub = refs[11:17]
    otok = refs[17:20]
    ltok = refs[20:23]

    for g, (_, d) in enumerate(DIL_CONFIG):
        q_ref, k_ref, v_ref = qkv_refs[3 * g:3 * g + 3]
        sub_len = seq // d
        nb = sub_len // DIL_QB
        seg = sub_len + 2 * DIL_RADIUS
        assert nb * DIL_QB == sub_len and nb & (nb - 1) == 0

        kp[...] = jnp.zeros_like(kp)
        vp[...] = jnp.zeros_like(vp)
        if d == 1:
            qs[...] = q_ref[...]
            kp[DIL_RADIUS:DIL_RADIUS + seq, :] = k_ref[...]
            vp[DIL_RADIUS:DIL_RADIUS + seq, :] = v_ref[...]
        else:
            x32[...] = q_ref[...].astype(F32)
            for res in range(d):
                qs[res * sub_len:(res + 1) * sub_len, :] = x32[pl.ds(res, sub_len, stride=d), :].astype(BF16)
            for src, dst in ((k_ref, kp), (v_ref, vp)):
                x32[...] = src[...].astype(F32)
                for res in range(d):
                    lo = res * seg + DIL_RADIUS
                    dst[lo:lo + sub_len, :] = x32[pl.ds(res, sub_len, stride=d), :].astype(BF16)

        def body(it, carry, g=g, nb=nb):
            nn = [it * ATTN_UNROLL + u for u in range(ATTN_UNROLL)]
            res = [lax.shift_right_logical(n, nb.bit_length() - 1) for n in nn]
            qoff = [pl.multiple_of(n * DIL_QB, DIL_QB) for n in nn]
            koff = [pl.multiple_of((n + r) * DIL_QB, DIL_QB) for n, r in zip(nn, res)]
            s = [lax.dot_general(qs[pl.ds(qoff[u], DIL_QB), :], kp[pl.ds(koff[u], DIL_KB), :],
                                 (((1,), (1,)), ((), ())), preferred_element_type=F32)
                 for u in range(ATTN_UNROLL)]
            kj = lax.broadcasted_iota(jnp.int32, (DIL_QB, DIL_KB), 1)
            p, den, lse = [], [], []
            for u in range(ATTN_UNROLL):
                i = nn[u] - res[u] * nb
                lo = jnp.where(i == 0, DIL_RADIUS, 0)
                hi = jnp.where(i == nb - 1, DIL_QB + DIL_RADIUS, DIL_KB)
                su = jnp.where((kj >= lo) & (kj < hi), s[u] * SCALE + bias_ref[g], NEG_INF)
                m = jnp.max(su, axis=-1, keepdims=True)
                pu = jnp.exp(su - m)
                du = jnp.sum(pu, axis=-1, keepdims=True)
                p.append(pu.astype(BF16))
                den.append(du)
                lse.append(m + jnp.log(du))
            o = [jnp.dot(p[u], vp[pl.ds(koff[u], DIL_KB), :], preferred_element_type=F32)
                 for u in range(ATTN_UNROLL)]
            for u in range(ATTN_UNROLL):
                osub[pl.ds(qoff[u], DIL_QB), :] = o[u] / den[u]
                lsub[pl.ds(qoff[u], DIL_QB), :] = jnp.broadcast_to(lse[u], (DIL_QB, HEAD_DIM))
            return carry

        assert (seq // DIL_QB) % ATTN_UNROLL == 0
        lax.fori_loop(0, seq // (DIL_QB * ATTN_UNROLL), body, 0)

        if d == 1:
            otok[g][...] = osub[...]
            ltok[g][...] = lsub[...]
        else:
            for res in range(d):
                otok[g][pl.ds(res, sub_len, stride=d), :] = osub[res * sub_len:(res + 1) * sub_len, :]
                ltok[g][pl.ds(res, sub_len, stride=d), :] = lsub[res * sub_len:(res + 1) * sub_len, :]

    chunk = 256

    def merge(c, carry):
        rows = pl.ds(pl.multiple_of(c * chunk, chunk), chunk)
        l0, l1, l2 = ltok[0][rows, :], ltok[1][rows, :], ltok[2][rows, :]
        m = jnp.maximum(jnp.maximum(l0, l1), l2)
        e0, e1, e2 = jnp.exp(l0 - m), jnp.exp(l1 - m), jnp.exp(l2 - m)
        num = e0 * otok[0][rows, :] + e1 * otok[1][rows, :] + e2 * otok[2][rows, :]
        o_ref[rows, :] = (num / (e0 + e1 + e2)).astype(o_ref.dtype)
        return carry

    lax.fori_loop(0, seq // chunk, merge, 0)


def _t5_bucket(rel):
    half = N_BUCKETS // 2
    max_exact = half // 2
    sign = jnp.where(rel > 0, half, 0)
    n = jnp.abs(rel)
    nf = jnp.maximum(n, 1).astype(F32)
    large = max_exact + (jnp.log(nf / max_exact) / math.log(MAX_DISTANCE / max_exact)
                         * (half - max_exact)).astype(jnp.int32)
    large = jnp.minimum(large, half - 1)
    return sign + jnp.where(n < max_exact, n, large)


def _dil_bias_tiles(rel_bias):
    rel = jnp.arange(-DIL_RADIUS, DIL_RADIUS + 1)
    band = []
    for g, (window, d) in enumerate(DIL_CONFIG):
        assert window // (2 * d) == DIL_RADIUS
        band.append(rel_bias[_t5_bucket(rel * d)][:, g * DIL_HEADS:(g + 1) * DIL_HEADS].T)
    band = jnp.stack(band, axis=0).astype(F32)
    lead = band.shape[:-1]
    f = jnp.concatenate([jnp.full(lead + (DIL_QB - 1,), NEG_INF, F32), band,
                         jnp.full(lead + (DIL_QB,), NEG_INF, F32)], axis=-1)
    return _toeplitz(f, DIL_QB, DIL_KB, DIL_QB - 1)


def _dil_attention(qkv, tiles, *, batch, seq):
    max_d = max(d for _, d in DIL_CONFIG)
    assert seq % (max_d * DIL_QB) == 0
    blk = (None, seq, HEAD_DIM)

    def col(g, s):
        return lambda b, h: (b, 0, (3 * g + s) * DIL_HEADS + h)

    in_specs = [pl.BlockSpec(blk, col(g, s)) for g in range(N_GROUPS) for s in range(3)]
    in_specs.append(pl.BlockSpec((N_GROUPS, None, DIL_QB, DIL_KB), lambda b, h: (0, h, 0, 0)))
    pad_rows = seq + 2 * DIL_RADIUS * max_d
    scratch = [
        pltpu.VMEM((seq, HEAD_DIM), F32),
        pltpu.VMEM((seq, HEAD_DIM), BF16),
        pltpu.VMEM((pad_rows, HEAD_DIM), BF16),
        pltpu.VMEM((pad_rows, HEAD_DIM), BF16),
        pltpu.VMEM((seq, HEAD_DIM), F32),
        pltpu.VMEM((seq, HEAD_DIM), F32),
    ]
    scratch += [pltpu.VMEM((seq, HEAD_DIM), F32) for _ in range(2 * N_GROUPS)]
    return pl.pallas_call(
        functools.partial(_dil_kernel, seq=seq),
        out_shape=jax.ShapeDtypeStruct((batch, seq, DIL_HEADS * HEAD_DIM), BF16),
        grid=(batch, DIL_HEADS),
        in_specs=in_specs,
        out_specs=pl.BlockSpec(blk, lambda b, h: (b, 0, h)),
        scratch_shapes=scratch,
        compiler_params=_params("parallel", "parallel"),
        name="dil_attention",
    )(*([qkv] * 9), tiles)


def _trunk(x, p):
    batch, seq, _ = x.shape
    x = x.reshape(batch * seq, D_MODEL)
    for i in range(DEPTH):
        j = i // N_MIXERS
        x = _ffn(x, p["wg"][i, 0], p["wu"][i, 0], p["wd"][i, 0], p["ln_g"][i, 0], p["ln_b"][i, 0])
        if i % N_MIXERS == 0:
            qkv = _proj(x, p["na_qkv"][j]).reshape(batch, seq, -1)
            a = _na_attention(qkv, p["na_slabs"][j], batch=batch, seq=seq)
            w_o = p["na_o"][j]
        else:
            qkv = _proj(x, p["dil_qkv"][j]).reshape(batch, seq, -1)
            a = _dil_attention(qkv, p["dil_tiles"], batch=batch, seq=seq)
            w_o = p["dil_o"][j]
        x = _oproj(a.reshape(batch * seq, -1), w_o, x, p["ln_g"][i, 1], p["ln_b"][i, 1])
        x = _ffn(x, p["wg"][i, 1], p["wu"][i, 1], p["wd"][i, 1], p["ln_g"][i, 2], p["ln_b"][i, 2])
    return x.reshape(batch, seq, D_MODEL)


def kernel(x_prompt, x_sample, ln_g, ln_b, ffn_w_gate, ffn_w_up, ffn_w_down, na_w_qkv, na_w_o, na_rpb, dil_w_qkv, dil_w_o, rel_bias):
    p = {
        "ln_g": ln_g.astype(F32).reshape(DEPTH, 3, 1, D_MODEL),
        "ln_b": ln_b.astype(F32).reshape(DEPTH, 3, 1, D_MODEL),
        "wg": ffn_w_gate.astype(BF16),
        "wu": ffn_w_up.astype(BF16),
        "wd": ffn_w_down.astype(BF16),
        "na_qkv": na_w_qkv.astype(BF16),
        "na_o": na_w_o.astype(BF16),
        "na_slabs": jax.vmap(_na_bias_slabs)(na_rpb),
        "dil_qkv": dil_w_qkv.astype(BF16),
        "dil_o": dil_w_o.astype(BF16),
        "dil_tiles": _dil_bias_tiles(rel_bias),
    }
    return (_trunk(x_prompt, p), _trunk(x_sample, p))
```

```python
import functools
import math

import jax
import jax.numpy as jnp
import numpy as np
from jax import lax
from jax.experimental import pallas as pl
from jax.experimental.pallas import tpu as pltpu

D_MODEL = 2048
DEPTH = 4
GRID_W = 64
HEAD_DIM = 128
N_MIXERS = 2
NA_HEADS = D_MODEL // HEAD_DIM
NA_ROWS = 8
NA_COLS = 16
DIL_CONFIG = ((128, 1), (512, 4), (2048, 16))
N_GROUPS = len(DIL_CONFIG)
DIL_HEADS = D_MODEL // (2 * HEAD_DIM)
DIL_RADIUS = 64
N_BUCKETS = 32
MAX_DISTANCE = 1024
D_FF = 5632
ALPHA = (2 * DEPTH) ** 0.25
LN_EPS = 1e-5
NEG_INF = -1e30
SCALE = HEAD_DIM ** -0.5

VMEM_LIMIT_BYTES = 56 * 1024 * 1024

BF16 = jnp.bfloat16
F32 = jnp.float32

DIL_QB = 128
DIL_KB = DIL_QB + 2 * DIL_RADIUS

ATTN_UNROLL = 8


def _layer_norm(y, g, b):
    mu = jnp.mean(y, axis=-1, keepdims=True)
    yc = y - mu
    var = jnp.mean(yc * yc, axis=-1, keepdims=True)
    return yc * lax.rsqrt(var + LN_EPS) * g + b


def _params(*semantics):
    return pltpu.CompilerParams(dimension_semantics=semantics, vmem_limit_bytes=VMEM_LIMIT_BYTES)


def _ffn_kernel(x_ref, wg_ref, wu_ref, wd_ref, g_ref, b_ref, o_ref, xb_ref):
    k = pl.program_id(1)

    @pl.when(k == 0)
    def _init():
        xb_ref[...] = x_ref[...].astype(BF16)
        o_ref[...] = jnp.zeros_like(o_ref)

    xb = xb_ref[...]
    gate = jnp.dot(xb, wg_ref[...], preferred_element_type=F32)
    up = jnp.dot(xb, wu_ref[...], preferred_element_type=F32)
    h = (gate * (1.0 / (1.0 + jnp.exp(-gate))) * up).astype(BF16)
    o_ref[...] += jnp.dot(h, wd_ref[...], preferred_element_type=F32)

    @pl.when(k == pl.num_programs(1) - 1)
    def _finish():
        y = ALPHA * x_ref[...] + 0.5 * o_ref[...]
        o_ref[...] = _layer_norm(y, g_ref[...], b_ref[...])


def _ln_spec(layer, idx):
    return pl.BlockSpec((None, None, 1, D_MODEL), lambda *_: (layer, idx, 0, 0))


def _ffn(x, p, layer, half, *, tm=512, tf=512):
    m = x.shape[0]
    assert m % tm == 0 and D_FF % tf == 0
    return pl.pallas_call(
        _ffn_kernel,
        out_shape=jax.ShapeDtypeStruct((m, D_MODEL), F32),
        grid=(m // tm, D_FF // tf),
        in_specs=[
            pl.BlockSpec((tm, D_MODEL), lambda i, k: (i, 0)),
            pl.BlockSpec((None, None, D_MODEL, tf), lambda i, k: (layer, half, 0, k)),
            pl.BlockSpec((None, None, D_MODEL, tf), lambda i, k: (layer, half, 0, k)),
            pl.BlockSpec((None, None, tf, D_MODEL), lambda i, k: (layer, half, k, 0)),
            _ln_spec(layer, 2 * half),
            _ln_spec(layer, 2 * half),
        ],
        out_specs=pl.BlockSpec((tm, D_MODEL), lambda i, k: (i, 0)),
        scratch_shapes=[pltpu.VMEM((tm, D_MODEL), BF16)],
        compiler_params=_params("parallel", "arbitrary"),
        name="ffn",
    )(x, p["wg"], p["wu"], p["wd"], p["ln_g"], p["ln_b"])


def _proj_kernel(x_ref, w_ref, o_ref, xb_ref):
    @pl.when(pl.program_id(1) == 0)
    def _init():
        xb_ref[...] = x_ref[...].astype(BF16)

    o_ref[...] = jnp.dot(xb_ref[...], w_ref[...], preferred_element_type=F32).astype(o_ref.dtype)


def _proj(x, w, idx, *, tm=1024, tn=1024):
    m, n = x.shape[0], w.shape[2]
    assert m % tm == 0 and n % tn == 0
    return pl.pallas_call(
        _proj_kernel,
        out_shape=jax.ShapeDtypeStruct((m, n), BF16),
        grid=(m // tm, n // tn),
        in_specs=[
            pl.BlockSpec((tm, D_MODEL), lambda i, j: (i, 0)),
            pl.BlockSpec((None, D_MODEL, tn), lambda i, j: (idx, 0, j)),
        ],
        out_specs=pl.BlockSpec((tm, tn), lambda i, j: (i, j)),
        scratch_shapes=[pltpu.VMEM((tm, D_MODEL), BF16)],
        compiler_params=_params("parallel", "arbitrary"),
        name="qkv_proj",
    )(x, w)


def _oproj_kernel(a_ref, w_ref, x_ref, g_ref, b_ref, o_ref):
    mix = jnp.dot(a_ref[...], w_ref[...], preferred_element_type=F32)
    y = ALPHA * x_ref[...] + mix
    o_ref[...] = _layer_norm(y, g_ref[...], b_ref[...])


def _oproj(a, w, idx, x, p, layer, *, tm=512):
    m, kdim = a.shape
    assert m % tm == 0
    return pl.pallas_call(
        _oproj_kernel,
        out_shape=jax.ShapeDtypeStruct((m, D_MODEL), F32),
        grid=(m // tm,),
        in_specs=[
            pl.BlockSpec((tm, kdim), lambda i: (i, 0)),
            pl.BlockSpec((None, kdim, D_MODEL), lambda i: (idx, 0, 0)),
            pl.BlockSpec((tm, D_MODEL), lambda i: (i, 0)),
            _ln_spec(layer, 1),
            _ln_spec(layer, 1),
        ],
        out_specs=pl.BlockSpec((tm, D_MODEL), lambda i: (i, 0)),
        compiler_params=_params("parallel"),
        name="out_proj",
    )(a, w, x, p["ln_g"], p["ln_b"])


def _na_kernel(q_ref, k_ref, v_ref, bias_ref, o_ref, *, rows):
    kw = NA_ROWS * GRID_W

    def body(it, carry):
        rr = [it * ATTN_UNROLL + u for u in range(ATTN_UNROLL)]
        rs = [jnp.clip(r - NA_ROWS // 2, 0, rows - NA_ROWS) for r in rr]
        kstart = [pl.multiple_of(x * GRID_W, GRID_W) for x in rs]
        qstart = [pl.multiple_of(r * GRID_W, GRID_W) for r in rr]
        s = [lax.dot_general(q_ref[pl.ds(qstart[u], GRID_W), :], k_ref[pl.ds(kstart[u], kw), :],
                             (((1,), (1,)), ((), ())), preferred_element_type=F32)
             for u in range(ATTN_UNROLL)]
        p, den = [], []
        for u in range(ATTN_UNROLL):
            su = s[u] * SCALE + bias_ref[rs[u] - rr[u] + (NA_ROWS - 1)]
            pu = jnp.exp(su - jnp.max(su, axis=-1, keepdims=True))
            den.append(jnp.sum(pu, axis=-1, keepdims=True))
            p.append(pu.astype(BF16))
        o = [jnp.dot(p[u], v_ref[pl.ds(kstart[u], kw), :], preferred_element_type=F32)
             for u in range(ATTN_UNROLL)]
        for u in range(ATTN_UNROLL):
            o_ref[pl.ds(qstart[u], GRID_W), :] = (o[u] / den[u]).astype(o_ref.dtype)
        return carry

    assert rows % ATTN_UNROLL == 0
    lax.fori_loop(0, rows // ATTN_UNROLL, body, 0)


def _toeplitz(f, n_rows, n_cols, offset):
    p = f.shape[-1]
    assert offset - (n_rows - 1) >= 0 and offset + n_cols <= p - 1
    lead = f.shape[:-1]
    x = jnp.broadcast_to(f[..., None, :], lead + (n_rows, p)).reshape(lead + (n_rows * p,))
    y = x[..., :n_rows * (p - 1)].reshape(lead + (n_rows, p - 1))
    return y[..., offset:offset + n_cols]


def _na_bias_slabs(rpb):
    h = rpb.shape[0]
    c = np.arange(GRID_W)
    col_start = np.clip(c - NA_COLS // 2, 0, GRID_W - NA_COLS)
    col_ok = (c[None, :] >= col_start[:, None]) & (c[None, :] < col_start[:, None] + NA_COLS)
    edge = GRID_W - NA_COLS
    f = jnp.concatenate([jnp.broadcast_to(rpb[..., :1], rpb.shape[:-1] + (edge,)), rpb,
                         jnp.broadcast_to(rpb[..., -1:], rpb.shape[:-1] + (edge + 1,))], axis=-1).astype(F32)
    table = jnp.where(col_ok, _toeplitz(f, GRID_W, GRID_W, GRID_W - 1), NEG_INF)
    slabs = jnp.stack([table[:, d0:d0 + NA_ROWS] for d0 in range(NA_ROWS)], axis=1)
    return slabs.transpose(0, 1, 3, 2, 4).reshape(h, NA_ROWS, GRID_W, NA_ROWS * GRID_W)


def _na_attention(qkv, slabs, idx, *, batch, seq):
    rows = seq // GRID_W
    assert rows >= NA_ROWS and seq % GRID_W == 0
    blk = (None, seq, HEAD_DIM)
    return pl.pallas_call(
        functools.partial(_na_kernel, rows=rows),
        out_shape=jax.ShapeDtypeStruct((batch, seq, NA_HEADS * HEAD_DIM), BF16),
        grid=(batch, NA_HEADS),
        in_specs=[
            pl.BlockSpec(blk, lambda b, h: (b, 0, h)),
            pl.BlockSpec(blk, lambda b, h: (b, 0, NA_HEADS + h)),
            pl.BlockSpec(blk, lambda b, h: (b, 0, 2 * NA_HEADS + h)),
            pl.BlockSpec((None, None, NA_ROWS, GRID_W, NA_ROWS * GRID_W), lambda b, h: (idx, h, 0, 0, 0)),
        ],
        out_specs=pl.BlockSpec(blk, lambda b, h: (b, 0, h)),
        compiler_params=_params("parallel", "parallel"),
        name="na_attention",
    )(qkv, qkv, qkv, slabs)


def _dil_kernel(*refs, seq):
    qkv_refs = refs[:9]
    bias_ref, o_ref = refs[9], refs[10]
    x32, qs, kp, vp = refs[11:15]
    otok = refs[15:18]
    ltok = refs[18:21]

    for g, (_, d) in enumerate(DIL_CONFIG):
        q_ref, k_ref, v_ref = qkv_refs[3 * g:3 * g + 3]
        sub_len = seq // d
        nb = sub_len // DIL_QB
        seg = sub_len + 2 * DIL_RADIUS
        assert nb * DIL_QB == sub_len and nb & (nb - 1) == 0

        for dst in (kp, vp):
            dst[0:DIL_RADIUS, :] = jnp.zeros((DIL_RADIUS, HEAD_DIM), BF16)
            for res in range(d):
                hi = res * seg + DIL_RADIUS + sub_len
                dst[hi:hi + 2 * DIL_RADIUS, :] = jnp.zeros((2 * DIL_RADIUS, HEAD_DIM), BF16)
        if d == 1:
            qs[...] = q_ref[...]
            kp[DIL_RADIUS:DIL_RADIUS + seq, :] = k_ref[...]
            vp[DIL_RADIUS:DIL_RADIUS + seq, :] = v_ref[...]
        else:
            x32[...] = q_ref[...].astype(F32)
            for res in range(d):
                qs[res * sub_len:(res + 1) * sub_len, :] = x32[pl.ds(res, sub_len, stride=d), :].astype(BF16)
            for src, dst in ((k_ref, kp), (v_ref, vp)):
                x32[...] = src[...].astype(F32)
                for res in range(d):
                    lo = res * seg + DIL_RADIUS
                    dst[lo:lo + sub_len, :] = x32[pl.ds(res, sub_len, stride=d), :].astype(BF16)

        def body(it, carry, g=g, d=d, nb=nb):
            nn = [it * ATTN_UNROLL + u for u in range(ATTN_UNROLL)]
            res = [lax.shift_right_logical(n, nb.bit_length() - 1) for n in nn]
            blk = [n - r * nb for n, r in zip(nn, res)]
            qoff = [pl.multiple_of(n * DIL_QB, DIL_QB) for n in nn]
            koff = [pl.multiple_of((n + r) * DIL_QB, DIL_QB) for n, r in zip(nn, res)]
            s = [lax.dot_general(qs[pl.ds(qoff[u], DIL_QB), :], kp[pl.ds(koff[u], DIL_KB), :],
                                 (((1,), (1,)), ((), ())), preferred_element_type=F32)
                 for u in range(ATTN_UNROLL)]
            p, den, lse = [], [], []
            for u in range(ATTN_UNROLL):
                edge = jnp.where(blk[u] == 0, 1, 0) + jnp.where(blk[u] == nb - 1, 2, 0)
                su = s[u] * SCALE + bias_ref[g, edge]
                m = jnp.max(su, axis=-1, keepdims=True)
                pu = jnp.exp(su - m)
                du = jnp.sum(pu, axis=-1, keepdims=True)
                p.append(pu.astype(BF16))
                den.append(du)
                lse.append(m + jnp.log(du))
            o = [jnp.dot(p[u], vp[pl.ds(koff[u], DIL_KB), :], preferred_element_type=F32)
                 for u in range(ATTN_UNROLL)]
            for u in range(ATTN_UNROLL):
                if d == 1:
                    rows = pl.ds(qoff[u], DIL_QB)
                else:
                    rows = pl.ds(res[u] + blk[u] * (DIL_QB * d), DIL_QB, stride=d)
                otok[g][rows, :] = o[u] / den[u]
                ltok[g][rows, :] = jnp.broadcast_to(lse[u], (DIL_QB, HEAD_DIM))
            return carry

        assert (seq // DIL_QB) % ATTN_UNROLL == 0
        lax.fori_loop(0, seq // (DIL_QB * ATTN_UNROLL), body, 0)

    chunk = 256

    def merge(c, carry):
        rows = pl.ds(pl.multiple_of(c * chunk, chunk), chunk)
        l0, l1, l2 = ltok[0][rows, :], ltok[1][rows, :], ltok[2][rows, :]
        m = jnp.maximum(jnp.maximum(l0, l1), l2)
        e0, e1, e2 = jnp.exp(l0 - m), jnp.exp(l1 - m), jnp.exp(l2 - m)
        num = e0 * otok[0][rows, :] + e1 * otok[1][rows, :] + e2 * otok[2][rows, :]
        o_ref[rows, :] = (num / (e0 + e1 + e2)).astype(o_ref.dtype)
        return carry

    lax.fori_loop(0, seq // chunk, merge, 0)


def _t5_bucket(rel):
    half = N_BUCKETS // 2
    max_exact = half // 2
    sign = jnp.where(rel > 0, half, 0)
    n = jnp.abs(rel)
    nf = jnp.maximum(n, 1).astype(F32)
    large = max_exact + (jnp.log(nf / max_exact) / math.log(MAX_DISTANCE / max_exact)
                         * (half - max_exact)).astype(jnp.int32)
    large = jnp.minimum(large, half - 1)
    return sign + jnp.where(n < max_exact, n, large)


def _dil_bias_tiles(rel_bias):
    rel = jnp.arange(-DIL_RADIUS, DIL_RADIUS + 1)
    band = []
    for g, (window, d) in enumerate(DIL_CONFIG):
        assert window // (2 * d) == DIL_RADIUS
        band.append(rel_bias[_t5_bucket(rel * d)][:, g * DIL_HEADS:(g + 1) * DIL_HEADS].T)
    band = jnp.stack(band, axis=0).astype(F32)
    lead = band.shape[:-1]
    f = jnp.concatenate([jnp.full(lead + (DIL_QB - 1,), NEG_INF, F32), band,
                         jnp.full(lead + (DIL_QB,), NEG_INF, F32)], axis=-1)
    tile = _toeplitz(f, DIL_QB, DIL_KB, DIL_QB - 1)
    kj = np.arange(DIL_KB)
    off_seq = np.stack([((e & 1) != 0) & (kj < DIL_RADIUS) | ((e & 2) != 0) & (kj >= DIL_QB + DIL_RADIUS)
                        for e in range(4)])
    return jnp.where(off_seq[None, :, None, None, :], NEG_INF, tile[:, None])


def _dil_attention(qkv, tiles, *, batch, seq):
    max_d = max(d for _, d in DIL_CONFIG)
    assert seq % (max_d * DIL_QB) == 0
    blk = (None, seq, HEAD_DIM)

    def col(g, s):
        return lambda b, h: (b, 0, (3 * g + s) * DIL_HEADS + h)

    in_specs = [pl.BlockSpec(blk, col(g, s)) for g in range(N_GROUPS) for s in range(3)]
    in_specs.append(pl.BlockSpec((N_GROUPS, 4, None, DIL_QB, DIL_KB), lambda b, h: (0, 0, h, 0, 0)))
    pad_rows = seq + 2 * DIL_RADIUS * max_d + DIL_RADIUS
    scratch = [
        pltpu.VMEM((seq, HEAD_DIM), F32),
        pltpu.VMEM((seq, HEAD_DIM), BF16),
        pltpu.VMEM((pad_rows, HEAD_DIM), BF16),
        pltpu.VMEM((pad_rows, HEAD_DIM), BF16),
    ]
    scratch += [pltpu.VMEM((seq, HEAD_DIM), F32) for _ in range(2 * N_GROUPS)]
    return pl.pallas_call(
        functools.partial(_dil_kernel, seq=seq),
        out_shape=jax.ShapeDtypeStruct((batch, seq, DIL_HEADS * HEAD_DIM), BF16),
        grid=(batch, DIL_HEADS),
        in_specs=in_specs,
        out_specs=pl.BlockSpec(blk, lambda b, h: (b, 0, h)),
        scratch_shapes=scratch,
        compiler_params=_params("parallel", "parallel"),
        name="dil_attention",
    )(*([qkv] * 9), tiles)


def _trunk(x, p):
    batch, seq, _ = x.shape
    x = x.reshape(batch * seq, D_MODEL)
    for i in range(DEPTH):
        j = i // N_MIXERS
        x = _ffn(x, p, i, 0)
        if i % N_MIXERS == 0:
            qkv = _proj(x, p["na_qkv"], j).reshape(batch, seq, -1)
            a = _na_attention(qkv, p["na_slabs"], j, batch=batch, seq=seq)
            w_o = p["na_o"]
        else:
            qkv = _proj(x, p["dil_qkv"], j).reshape(batch, seq, -1)
            a = _dil_attention(qkv, p["dil_tiles"], batch=batch, seq=seq)
            w_o = p["dil_o"]
        x = _oproj(a.reshape(batch * seq, -1), w_o, j, x, p, i)
        x = _ffn(x, p, i, 1)
    return x.reshape(batch, seq, D_MODEL)


def kernel(x_prompt, x_sample, ln_g, ln_b, ffn_w_gate, ffn_w_up, ffn_w_down, na_w_qkv, na_w_o, na_rpb, dil_w_qkv, dil_w_o, rel_bias):
    p = {
        "ln_g": ln_g.astype(F32).reshape(DEPTH, 3, 1, D_MODEL),
        "ln_b": ln_b.astype(F32).reshape(DEPTH, 3, 1, D_MODEL),
        "wg": ffn_w_gate.astype(BF16),
        "wu": ffn_w_up.astype(BF16),
        "wd": ffn_w_down.astype(BF16),
        "na_qkv": na_w_qkv.astype(BF16),
        "na_o": na_w_o.astype(BF16),
        "na_slabs": jax.vmap(_na_bias_slabs)(na_rpb),
        "dil_qkv": dil_w_qkv.astype(BF16),
        "dil_o": dil_w_o.astype(BF16),
        "dil_tiles": _dil_bias_tiles(rel_bias),
    }
    return (_trunk(x_prompt, p), _trunk(x_sample, p))
```

```python
import functools
import math

import jax
import jax.numpy as jnp
import numpy as np
from jax import lax
from jax.experimental import pallas as pl
from jax.experimental.pallas import tpu as pltpu

D_MODEL = 2048
DEPTH = 4
GRID_W = 64
HEAD_DIM = 128
N_MIXERS = 2
NA_HEADS = D_MODEL // HEAD_DIM
NA_ROWS = 8
NA_COLS = 16
DIL_CONFIG = ((128, 1), (512, 4), (2048, 16))
N_GROUPS = len(DIL_CONFIG)
DIL_HEADS = D_MODEL // (2 * HEAD_DIM)
DIL_RADIUS = 64
N_BUCKETS = 32
MAX_DISTANCE = 1024
D_FF = 5632
ALPHA = (2 * DEPTH) ** 0.25
LN_EPS = 1e-5
NEG_INF = -1e30
SCALE = HEAD_DIM ** -0.5
LOG2E = math.log2(math.e)

VMEM_LIMIT_BYTES = 56 * 1024 * 1024

BF16 = jnp.bfloat16
F32 = jnp.float32

DIL_QB = 128
DIL_KB = DIL_QB + 2 * DIL_RADIUS

ATTN_UNROLL = 8
NA_UNROLL = 16
NA_LOOKAHEAD = 16


def _layer_norm(y, g, b, eps=LN_EPS):
    mu = jnp.mean(y, axis=-1, keepdims=True)
    yc = y - mu
    var = jnp.mean(yc * yc, axis=-1, keepdims=True)
    return yc * lax.rsqrt(var + eps) * g + b


def _params(*semantics):
    return pltpu.CompilerParams(dimension_semantics=semantics, vmem_limit_bytes=VMEM_LIMIT_BYTES)


def _ffn_kernel(x_hbm, wg_ref, wu_ref, wd_ref, g_ref, b_ref, o_ref, x_stage, xb_ref, x_sem, *, tm):
    i = pl.program_id(0)
    k = pl.program_id(1)

    def x_copy(tile):
        return pltpu.make_async_copy(x_hbm.at[pl.ds(tile * tm, tm), :], x_stage, x_sem)

    @pl.when(k == 0)
    def _init():
        @pl.when(i == 0)
        def _first():
            x_copy(0).start()

        x_copy(i).wait()
        x = x_stage[...]
        xb_ref[...] = x.astype(BF16)
        o_ref[...] = (2.0 * ALPHA) * x

    @pl.when((k == 1) & (i + 1 < pl.num_programs(0)))
    def _prefetch():
        x_copy(i + 1).start()

    xb = xb_ref[...]
    gate = jnp.dot(xb, wg_ref[...], preferred_element_type=F32)
    up = jnp.dot(xb, wu_ref[...], preferred_element_type=F32)
    h = (gate * (1.0 / (1.0 + jnp.exp(-gate))) * up).astype(BF16)
    o_ref[...] += jnp.dot(h, wd_ref[...], preferred_element_type=F32)

    @pl.when(k == pl.num_programs(1) - 1)
    def _finish():
        o_ref[...] = _layer_norm(o_ref[...], g_ref[...], b_ref[...], eps=4.0 * LN_EPS)


def _ln_spec(layer, idx):
    return pl.BlockSpec((None, None, 1, D_MODEL), lambda *_: (layer, idx, 0, 0))


def _ffn(x, p, layer, half, *, tm=1024, tf=512):
    m = x.shape[0]
    assert m % tm == 0 and D_FF % tf == 0 and D_FF // tf >= 2
    return pl.pallas_call(
        functools.partial(_ffn_kernel, tm=tm),
        out_shape=jax.ShapeDtypeStruct((m, D_MODEL), F32),
        grid=(m // tm, D_FF // tf),
        in_specs=[
            pl.BlockSpec(memory_space=pl.ANY),
            pl.BlockSpec((None, None, D_MODEL, tf), lambda i, k: (layer, half, 0, k)),
            pl.BlockSpec((None, None, D_MODEL, tf), lambda i, k: (layer, half, 0, k)),
            pl.BlockSpec((None, None, tf, D_MODEL), lambda i, k: (layer, half, k, 0)),
            _ln_spec(layer, 2 * half),
            _ln_spec(layer, 2 * half),
        ],
        out_specs=pl.BlockSpec((tm, D_MODEL), lambda i, k: (i, 0)),
        scratch_shapes=[pltpu.VMEM((tm, D_MODEL), F32), pltpu.VMEM((tm, D_MODEL), BF16),
                        pltpu.SemaphoreType.DMA(())],
        compiler_params=_params("arbitrary", "arbitrary"),
        name="ffn",
    )(x, p["wg"], p["wu"], p["wd"], p["ln_g"], p["ln_b"])


def _proj_kernel(x_ref, w_ref, o_ref, xb_ref):
    @pl.when(pl.program_id(1) == 0)
    def _init():
        xb_ref[...] = x_ref[...].astype(BF16)

    o_ref[...] = jnp.dot(xb_ref[...], w_ref[...], preferred_element_type=F32).astype(o_ref.dtype)


def _proj(x, w, idx, *, tm=1024, tn=1024):
    m, n = x.shape[0], w.shape[2]
    assert m % tm == 0 and n % tn == 0
    return pl.pallas_call(
        _proj_kernel,
        out_shape=jax.ShapeDtypeStruct((m, n), BF16),
        grid=(m // tm, n // tn),
        in_specs=[
            pl.BlockSpec((tm, D_MODEL), lambda i, j: (i, 0)),
            pl.BlockSpec((None, D_MODEL, tn), lambda i, j: (idx, 0, j)),
        ],
        out_specs=pl.BlockSpec((tm, tn), lambda i, j: (i, j)),
        scratch_shapes=[pltpu.VMEM((tm, D_MODEL), BF16)],
        compiler_params=_params("parallel", "arbitrary"),
        name="qkv_proj",
    )(x, w)


def _oproj_kernel(a_ref, w_ref, x_ref, g_ref, b_ref, o_ref):
    mix = jnp.dot(a_ref[...], w_ref[...], preferred_element_type=F32)
    y = ALPHA * x_ref[...] + mix
    o_ref[...] = _layer_norm(y, g_ref[...], b_ref[...])


def _oproj(a, w, idx, x, p, layer, *, tm=512):
    m, kdim = a.shape
    assert m % tm == 0
    return pl.pallas_call(
        _oproj_kernel,
        out_shape=jax.ShapeDtypeStruct((m, D_MODEL), F32),
        grid=(m // tm,),
        in_specs=[
            pl.BlockSpec((tm, kdim), lambda i: (i, 0)),
            pl.BlockSpec((None, kdim, D_MODEL), lambda i: (idx, 0, 0)),
            pl.BlockSpec((tm, D_MODEL), lambda i: (i, 0)),
            _ln_spec(layer, 1),
            _ln_spec(layer, 1),
        ],
        out_specs=pl.BlockSpec((tm, D_MODEL), lambda i: (i, 0)),
        compiler_params=_params("parallel"),
        name="out_proj",
    )(a, w, x, p["ln_g"], p["ln_b"])


def _na_kernel(q_ref, k_ref, v_ref, bias_ref, o_ref, *, rows):
    kw = NA_ROWS * GRID_W

    def body(it, carry):
        rr = [it * NA_UNROLL + u for u in range(NA_UNROLL)]
        rs = [jnp.clip(r - NA_ROWS // 2, 0, rows - NA_ROWS) for r in rr]
        kstart = [pl.multiple_of(x * GRID_W, GRID_W) for x in rs]
        qstart = [pl.multiple_of(r * GRID_W, GRID_W) for r in rr]
        s = {}
        for step in range(NA_UNROLL + NA_LOOKAHEAD):
            if step < NA_UNROLL:
                s[step] = lax.dot_general(q_ref[pl.ds(qstart[step], GRID_W), :], k_ref[pl.ds(kstart[step], kw), :],
                                          (((1,), (1,)), ((), ())), preferred_element_type=F32)
            u = step - NA_LOOKAHEAD
            if u >= 0:
                su = s.pop(u) * (SCALE * LOG2E) + bias_ref[rs[u] - rr[u] + (NA_ROWS - 1)]
                pu = jnp.exp2(su - jnp.max(su, axis=-1, keepdims=True))
                den = jnp.sum(pu, axis=-1, keepdims=True)
                o = jnp.dot(pu.astype(BF16), v_ref[pl.ds(kstart[u], kw), :], preferred_element_type=F32)
                o_ref[pl.ds(qstart[u], GRID_W), :] = (o / den).astype(o_ref.dtype)
        return carry

    assert rows % NA_UNROLL == 0
    lax.fori_loop(0, rows // NA_UNROLL, body, 0)


def _toeplitz(f, n_rows, n_cols, offset):
    p = f.shape[-1]
    assert offset - (n_rows - 1) >= 0 and offset + n_cols <= p - 1
    lead = f.shape[:-1]
    x = jnp.broadcast_to(f[..., None, :], lead + (n_rows, p)).reshape(lead + (n_rows * p,))
    y = x[..., :n_rows * (p - 1)].reshape(lead + (n_rows, p - 1))
    return y[..., offset:offset + n_cols]


def _na_bias_slabs(rpb):
    h = rpb.shape[0]
    c = np.arange(GRID_W)
    col_start = np.clip(c - NA_COLS // 2, 0, GRID_W - NA_COLS)
    col_ok = (c[None, :] >= col_start[:, None]) & (c[None, :] < col_start[:, None] + NA_COLS)
    edge = GRID_W - NA_COLS
    f = jnp.concatenate([jnp.broadcast_to(rpb[..., :1], rpb.shape[:-1] + (edge,)), rpb,
                         jnp.broadcast_to(rpb[..., -1:], rpb.shape[:-1] + (edge + 1,))], axis=-1).astype(F32)
    table = jnp.where(col_ok, _toeplitz(f, GRID_W, GRID_W, GRID_W - 1) * LOG2E, NEG_INF)
    slabs = jnp.stack([table[:, d0:d0 + NA_ROWS] for d0 in range(NA_ROWS)], axis=1)
    return slabs.transpose(0, 1, 3, 2, 4).reshape(h, NA_ROWS, GRID_W, NA_ROWS * GRID_W)


def _na_attention(qkv, slabs, idx, *, batch, seq):
    rows = seq // GRID_W
    assert rows >= NA_ROWS and seq % GRID_W == 0
    blk = (None, seq, HEAD_DIM)
    return pl.pallas_call(
        functools.partial(_na_kernel, rows=rows),
        out_shape=jax.ShapeDtypeStruct((batch, seq, NA_HEADS * HEAD_DIM), BF16),
        grid=(batch, NA_HEADS),
        in_specs=[
            pl.BlockSpec(blk, lambda b, h: (b, 0, h)),
            pl.BlockSpec(blk, lambda b, h: (b, 0, NA_HEADS + h)),
            pl.BlockSpec(blk, lambda b, h: (b, 0, 2 * NA_HEADS + h)),
            pl.BlockSpec((None, None, NA_ROWS, GRID_W, NA_ROWS * GRID_W), lambda b, h: (idx, h, 0, 0, 0)),
        ],
        out_specs=pl.BlockSpec(blk, lambda b, h: (b, 0, h)),
        compiler_params=_params("parallel", "parallel"),
        name="na_attention",
    )(qkv, qkv, qkv, slabs)


def _dil_kernel(*refs, seq):
    qkv_refs = refs[:9]
    bias_ref, o_ref = refs[9], refs[10]
    x32, qs, kp, vp = refs[11:15]
    otok = refs[15:18]
    ltok = refs[18:21]

    for g, (_, d) in enumerate(DIL_CONFIG):
        q_ref, k_ref, v_ref = qkv_refs[3 * g:3 * g + 3]
        sub_len = seq // d
        nb = sub_len // DIL_QB
        seg = sub_len + 2 * DIL_RADIUS
        assert nb * DIL_QB == sub_len and nb & (nb - 1) == 0

        for dst in (kp, vp):
            dst[0:DIL_RADIUS, :] = jnp.zeros((DIL_RADIUS, HEAD_DIM), BF16)
            for res in range(d):
                hi = res * seg + DIL_RADIUS + sub_len
                dst[hi:hi + 2 * DIL_RADIUS, :] = jnp.zeros((2 * DIL_RADIUS, HEAD_DIM), BF16)
        if d == 1:
            qs[...] = q_ref[...]
            kp[DIL_RADIUS:DIL_RADIUS + seq, :] = k_ref[...]
            vp[DIL_RADIUS:DIL_RADIUS + seq, :] = v_ref[...]
        else:
            x32[...] = q_ref[...].astype(F32)
            for res in range(d):
                qs[res * sub_len:(res + 1) * sub_len, :] = x32[pl.ds(res, sub_len, stride=d), :].astype(BF16)
            for src, dst in ((k_ref, kp), (v_ref, vp)):
                x32[...] = src[...].astype(F32)
                for res in range(d):
                    lo = res * seg + DIL_RADIUS
                    dst[lo:lo + sub_len, :] = x32[pl.ds(res, sub_len, stride=d), :].astype(BF16)

        def body(it, carry, g=g, d=d, nb=nb):
            nn = [it * ATTN_UNROLL + u for u in range(ATTN_UNROLL)]
            res = [lax.shift_right_logical(n, nb.bit_length() - 1) for n in nn]
            blk = [n - r * nb for n, r in zip(nn, res)]
            qoff = [pl.multiple_of(n * DIL_QB, DIL_QB) for n in nn]
            koff = [pl.multiple_of((n + r) * DIL_QB, DIL_QB) for n, r in zip(nn, res)]
            s = [lax.dot_general(qs[pl.ds(qoff[u], DIL_QB), :], kp[pl.ds(koff[u], DIL_KB), :],
                                 (((1,), (1,)), ((), ())), preferred_element_type=F32)
                 for u in range(ATTN_UNROLL)]
            p, den, lse = [], [], []
            for u in range(ATTN_UNROLL):
                edge = jnp.where(blk[u] == 0, 1, 0) + jnp.where(blk[u] == nb - 1, 2, 0)
                su = s[u] * SCALE + bias_ref[g, edge]
                m = jnp.max(su, axis=-1, keepdims=True)
                pu = jnp.exp(su - m)
                du = jnp.sum(pu, axis=-1, keepdims=True)
                p.append(pu.astype(BF16))
                den.append(du)
                lse.append(m + jnp.log(du))
            o = [jnp.dot(p[u], vp[pl.ds(koff[u], DIL_KB), :], preferred_element_type=F32)
                 for u in range(ATTN_UNROLL)]
            for u in range(ATTN_UNROLL):
                if d == 1:
                    rows = pl.ds(qoff[u], DIL_QB)
                else:
                    rows = pl.ds(res[u] + blk[u] * (DIL_QB * d), DIL_QB, stride=d)
                otok[g][rows, :] = o[u] / den[u]
                ltok[g][rows, :] = jnp.broadcast_to(lse[u], (DIL_QB, HEAD_DIM))
            return carry

        assert (seq // DIL_QB) % ATTN_UNROLL == 0
        lax.fori_loop(0, seq // (DIL_QB * ATTN_UNROLL), body, 0)

    chunk = 256

    def merge(c, carry):
        rows = pl.ds(pl.multiple_of(c * chunk, chunk), chunk)
        l0, l1, l2 = ltok[0][rows, :], ltok[1][rows, :], ltok[2][rows, :]
        m = jnp.maximum(jnp.maximum(l0, l1), l2)
        e0, e1, e2 = jnp.exp(l0 - m), jnp.exp(l1 - m), jnp.exp(l2 - m)
        num = e0 * otok[0][rows, :] + e1 * otok[1][rows, :] + e2 * otok[2][rows, :]
        o_ref[rows, :] = (num / (e0 + e1 + e2)).astype(o_ref.dtype)
        return carry

    lax.fori_loop(0, seq // chunk, merge, 0)


def _t5_bucket(rel):
    half = N_BUCKETS // 2
    max_exact = half // 2
    sign = jnp.where(rel > 0, half, 0)
    n = jnp.abs(rel)
    nf = jnp.maximum(n, 1).astype(F32)
    large = max_exact + (jnp.log(nf / max_exact) / math.log(MAX_DISTANCE / max_exact)
                         * (half - max_exact)).astype(jnp.int32)
    large = jnp.minimum(large, half - 1)
    return sign + jnp.where(n < max_exact, n, large)


def _dil_bias_tiles(rel_bias):
    rel = jnp.arange(-DIL_RADIUS, DIL_RADIUS + 1)
    band = []
    for g, (window, d) in enumerate(DIL_CONFIG):
        assert window // (2 * d) == DIL_RADIUS
        band.append(rel_bias[_t5_bucket(rel * d)][:, g * DIL_HEADS:(g + 1) * DIL_HEADS].T)
    band = jnp.stack(band, axis=0).astype(F32)
    lead = band.shape[:-1]
    f = jnp.concatenate([jnp.full(lead + (DIL_QB - 1,), NEG_INF, F32), band,
                         jnp.full(lead + (DIL_QB,), NEG_INF, F32)], axis=-1)
    tile = _toeplitz(f, DIL_QB, DIL_KB, DIL_QB - 1)
    kj = np.arange(DIL_KB)
    off_seq = np.stack([((e & 1) != 0) & (kj < DIL_RADIUS) | ((e & 2) != 0) & (kj >= DIL_QB + DIL_RADIUS)
                        for e in range(4)])
    return jnp.where(off_seq[None, :, None, None, :], NEG_INF, tile[:, None])


def _dil_attention(qkv, tiles, *, batch, seq):
    max_d = max(d for _, d in DIL_CONFIG)
    assert seq % (max_d * DIL_QB) == 0
    blk = (None, seq, HEAD_DIM)

    def col(g, s):
        return lambda b, h: (b, 0, (3 * g + s) * DIL_HEADS + h)

    in_specs = [pl.BlockSpec(blk, col(g, s)) for g in range(N_GROUPS) for s in range(3)]
    in_specs.append(pl.BlockSpec((N_GROUPS, 4, None, DIL_QB, DIL_KB), lambda b, h: (0, 0, h, 0, 0)))
    pad_rows = seq + 2 * DIL_RADIUS * max_d + DIL_RADIUS
    scratch = [
        pltpu.VMEM((seq, HEAD_DIM), F32),
        pltpu.VMEM((seq, HEAD_DIM), BF16),
        pltpu.VMEM((pad_rows, HEAD_DIM), BF16),
        pltpu.VMEM((pad_rows, HEAD_DIM), BF16),
    ]
    scratch += [pltpu.VMEM((seq, HEAD_DIM), F32) for _ in range(2 * N_GROUPS)]
    return pl.pallas_call(
        functools.partial(_dil_kernel, seq=seq),
        out_shape=jax.ShapeDtypeStruct((batch, seq, DIL_HEADS * HEAD_DIM), BF16),
        grid=(batch, DIL_HEADS),
        in_specs=in_specs,
        out_specs=pl.BlockSpec(blk, lambda b, h: (b, 0, h)),
        scratch_shapes=scratch,
        compiler_params=_params("parallel", "parallel"),
        name="dil_attention",
    )(*([qkv] * 9), tiles)


def _trunk(x, p):
    batch, seq, _ = x.shape
    x = x.reshape(batch * seq, D_MODEL)
    for i in range(DEPTH):
        j = i // N_MIXERS
        x = _ffn(x, p, i, 0)
        if i % N_MIXERS == 0:
            qkv = _proj(x, p["na_qkv"], j).reshape(batch, seq, -1)
            a = _na_attention(qkv, p["na_slabs"], j, batch=batch, seq=seq)
            w_o = p["na_o"]
        else:
            qkv = _proj(x, p["dil_qkv"], j).reshape(batch, seq, -1)
            a = _dil_attention(qkv, p["dil_tiles"], batch=batch, seq=seq)
            w_o = p["dil_o"]
        x = _oproj(a.reshape(batch * seq, -1), w_o, j, x, p, i)
        x = _ffn(x, p, i, 1)
    return x.reshape(batch, seq, D_MODEL)


def kernel(x_prompt, x_sample, ln_g, ln_b, ffn_w_gate, ffn_w_up, ffn_w_down, na_w_qkv, na_w_o, na_rpb, dil_w_qkv, dil_w_o, rel_bias):
    p = {
        "ln_g": ln_g.astype(F32).reshape(DEPTH, 3, 1, D_MODEL),
        "ln_b": ln_b.astype(F32).reshape(DEPTH, 3, 1, D_MODEL),
        "wg": ffn_w_gate.astype(BF16),
        "wu": ffn_w_up.astype(BF16),
        "wd": ffn_w_down.astype(BF16),
        "na_qkv": na_w_qkv.astype(BF16),
        "na_o": na_w_o.astype(BF16),
        "na_slabs": jax.vmap(_na_bias_slabs)(na_rpb),
        "dil_qkv": dil_w_qkv.astype(BF16),
        "dil_o": dil_w_o.astype(BF16),
        "dil_tiles": _dil_bias_tiles(rel_bias),
    }
    return (_trunk(x_prompt, p), _trunk(x_sample, p))
```

```python
import functools
import math

import jax
import jax.numpy as jnp
import numpy as np
from jax import lax
from jax.experimental import pallas as pl
from jax.experimental.pallas import tpu as pltpu

D_MODEL = 2048
DEPTH = 4
GRID_W = 64
HEAD_DIM = 128
N_MIXERS = 2
NA_HEADS = D_MODEL // HEAD_DIM
NA_ROWS = 8
NA_COLS = 16
DIL_CONFIG = ((128, 1), (512, 4), (2048, 16))
N_GROUPS = len(DIL_CONFIG)
DIL_HEADS = D_MODEL // (2 * HEAD_DIM)
DIL_RADIUS = 64
N_BUCKETS = 32
MAX_DISTANCE = 1024
D_FF = 5632
ALPHA = (2 * DEPTH) ** 0.25
LN_EPS = 1e-5
NEG_INF = -1e30
SCALE = HEAD_DIM ** -0.5
LOG2E = math.log2(math.e)

VMEM_LIMIT_BYTES = 56 * 1024 * 1024

BF16 = jnp.bfloat16
F32 = jnp.float32

FFN_ROW_BLOCK = 256
OPROJ_ROW_BLOCK = 128

DIL_QB = 128
DIL_KB = DIL_QB + 2 * DIL_RADIUS

ATTN_UNROLL = 8
NA_UNROLL = 32
NA_LOOKAHEAD = 16


def _layer_norm(y, g, b, eps=LN_EPS):
    mu = jnp.mean(y, axis=-1, keepdims=True)
    yc = y - mu
    var = jnp.mean(yc * yc, axis=-1, keepdims=True)
    return yc * lax.rsqrt(var + eps) * g + b


def _params(*semantics):
    return pltpu.CompilerParams(dimension_semantics=semantics, vmem_limit_bytes=VMEM_LIMIT_BYTES)


def _ffn_kernel(x_hbm, wg_ref, wu_ref, wd_ref, g_ref, b_ref, o_ref, x_stage, xb_ref, x_sem, *, tm):
    i = pl.program_id(0)
    k = pl.program_id(1)
    last_k = pl.num_programs(1) - 1

    def x_copy(tile):
        return pltpu.make_async_copy(x_hbm.at[pl.ds(tile * tm, tm), :], x_stage, x_sem)

    def step(first, last):
        row_block = FFN_ROW_BLOCK if first or last else tm
        for r in range(tm // row_block):
            rows = pl.ds(r * row_block, row_block)
            if first:
                x = x_stage[rows, :]
                xb = x.astype(BF16)
                xb_ref[rows, :] = xb
                z = (2.0 * ALPHA) * x
            else:
                xb = xb_ref[rows, :]
                z = o_ref[rows, :]
            gate = jnp.dot(xb, wg_ref[...], preferred_element_type=F32)
            up = jnp.dot(xb, wu_ref[...], preferred_element_type=F32)
            h = (gate * (1.0 / (1.0 + jnp.exp(-gate))) * up).astype(BF16)
            z = z + jnp.dot(h, wd_ref[...], preferred_element_type=F32)
            if last:
                z = _layer_norm(z, g_ref[...], b_ref[...], eps=4.0 * LN_EPS)
            o_ref[rows, :] = z

    @pl.when(k == 0)
    def _first_step():
        @pl.when(i == 0)
        def _first_tile():
            x_copy(0).start()

        x_copy(i).wait()
        step(True, False)

    @pl.when((k == 1) & (i + 1 < pl.num_programs(0)))
    def _prefetch():
        x_copy(i + 1).start()

    @pl.when((k > 0) & (k < last_k))
    def _middle_step():
        step(False, False)

    @pl.when(k == last_k)
    def _last_step():
        step(False, True)


def _ln_spec(layer, idx):
    return pl.BlockSpec((None, None, 1, D_MODEL), lambda *_: (layer, idx, 0, 0))


def _ffn(x, p, layer, half, *, tm=1024, tf=512):
    m = x.shape[0]
    assert m % tm == 0 and tm % FFN_ROW_BLOCK == 0 and D_FF % tf == 0 and D_FF // tf >= 3
    return pl.pallas_call(
        functools.partial(_ffn_kernel, tm=tm),
        out_shape=jax.ShapeDtypeStruct((m, D_MODEL), F32),
        grid=(m // tm, D_FF // tf),
        in_specs=[
            pl.BlockSpec(memory_space=pl.ANY),
            pl.BlockSpec((None, None, D_MODEL, tf), lambda i, k: (layer, half, 0, k)),
            pl.BlockSpec((None, None, D_MODEL, tf), lambda i, k: (layer, half, 0, k)),
            pl.BlockSpec((None, None, tf, D_MODEL), lambda i, k: (layer, half, k, 0)),
            _ln_spec(layer, 2 * half),
            _ln_spec(layer, 2 * half),
        ],
        out_specs=pl.BlockSpec((tm, D_MODEL), lambda i, k: (i, 0)),
        scratch_shapes=[pltpu.VMEM((tm, D_MODEL), F32), pltpu.VMEM((tm, D_MODEL), BF16),
                        pltpu.SemaphoreType.DMA(())],
        compiler_params=_params("arbitrary", "arbitrary"),
        name="ffn",
    )(x, p["wg"], p["wu"], p["wd"], p["ln_g"], p["ln_b"])


def _proj_kernel(x_ref, w_ref, o_ref, xb_ref):
    @pl.when(pl.program_id(1) == 0)
    def _init():
        xb_ref[...] = x_ref[...].astype(BF16)

    o_ref[...] = jnp.dot(xb_ref[...], w_ref[...], preferred_element_type=F32).astype(o_ref.dtype)


def _proj(x, w, idx, *, tm=1024, tn=1024):
    m, n = x.shape[0], w.shape[2]
    assert m % tm == 0 and n % tn == 0
    return pl.pallas_call(
        _proj_kernel,
        out_shape=jax.ShapeDtypeStruct((m, n), BF16),
        grid=(m // tm, n // tn),
        in_specs=[
            pl.BlockSpec((tm, D_MODEL), lambda i, j: (i, 0)),
            pl.BlockSpec((None, D_MODEL, tn), lambda i, j: (idx, 0, j)),
        ],
        out_specs=pl.BlockSpec((tm, tn), lambda i, j: (i, j)),
        scratch_shapes=[pltpu.VMEM((tm, D_MODEL), BF16)],
        compiler_params=_params("parallel", "arbitrary"),
        name="qkv_proj",
    )(x, w)


def _oproj_kernel(a_ref, w_ref, x_ref, g_ref, b_ref, o_ref):
    for r in range(a_ref.shape[0] // OPROJ_ROW_BLOCK):
        rows = pl.ds(r * OPROJ_ROW_BLOCK, OPROJ_ROW_BLOCK)
        mix = jnp.dot(a_ref[rows, :], w_ref[...], preferred_element_type=F32)
        y = ALPHA * x_ref[rows, :] + mix
        o_ref[rows, :] = _layer_norm(y, g_ref[...], b_ref[...])


def _oproj(a, w, idx, x, p, layer, *, tm=512):
    m, kdim = a.shape
    assert m % tm == 0 and tm % OPROJ_ROW_BLOCK == 0
    return pl.pallas_call(
        _oproj_kernel,
        out_shape=jax.ShapeDtypeStruct((m, D_MODEL), F32),
        grid=(m // tm,),
        in_specs=[
            pl.BlockSpec((tm, kdim), lambda i: (i, 0)),
            pl.BlockSpec((None, kdim, D_MODEL), lambda i: (idx, 0, 0)),
            pl.BlockSpec((tm, D_MODEL), lambda i: (i, 0)),
            _ln_spec(layer, 1),
            _ln_spec(layer, 1),
        ],
        out_specs=pl.BlockSpec((tm, D_MODEL), lambda i: (i, 0)),
        compiler_params=_params("parallel"),
        name="out_proj",
    )(a, w, x, p["ln_g"], p["ln_b"])


def _na_kernel(q_ref, k_ref, v_ref, bias_ref, o_ref, *, rows):
    kw = NA_ROWS * GRID_W

    def body(it, carry):
        rr = [it * NA_UNROLL + u for u in range(NA_UNROLL)]
        rs = [jnp.clip(r - NA_ROWS // 2, 0, rows - NA_ROWS) for r in rr]
        kstart = [pl.multiple_of(x * GRID_W, GRID_W) for x in rs]
        qstart = [pl.multiple_of(r * GRID_W, GRID_W) for r in rr]
        s = {}
        for step in range(NA_UNROLL + NA_LOOKAHEAD):
            if step < NA_UNROLL:
                s[step] = lax.dot_general(q_ref[pl.ds(qstart[step], GRID_W), :], k_ref[pl.ds(kstart[step], kw), :],
                                          (((1,), (1,)), ((), ())), preferred_element_type=F32)
            u = step - NA_LOOKAHEAD
            if u >= 0:
                su = s.pop(u) * (SCALE * LOG2E) + bias_ref[rs[u] - rr[u] + (NA_ROWS - 1)]
                pu = jnp.exp2(su - jnp.max(su, axis=-1, keepdims=True))
                den = jnp.sum(pu, axis=-1, keepdims=True)
                o = jnp.dot(pu.astype(BF16), v_ref[pl.ds(kstart[u], kw), :], preferred_element_type=F32)
                o_ref[pl.ds(qstart[u], GRID_W), :] = (o / den).astype(o_ref.dtype)
        return carry

    assert rows % NA_UNROLL == 0
    lax.fori_loop(0, rows // NA_UNROLL, body, 0)


def _toeplitz(f, n_rows, n_cols, offset):
    p = f.shape[-1]
    assert offset - (n_rows - 1) >= 0 and offset + n_cols <= p - 1
    lead = f.shape[:-1]
    x = jnp.broadcast_to(f[..., None, :], lead + (n_rows, p)).reshape(lead + (n_rows * p,))
    y = x[..., :n_rows * (p - 1)].reshape(lead + (n_rows, p - 1))
    return y[..., offset:offset + n_cols]


def _na_bias_slabs(rpb):
    h = rpb.shape[0]
    c = np.arange(GRID_W)
    col_start = np.clip(c - NA_COLS // 2, 0, GRID_W - NA_COLS)
    col_ok = (c[None, :] >= col_start[:, None]) & (c[None, :] < col_start[:, None] + NA_COLS)
    edge = GRID_W - NA_COLS
    f = jnp.concatenate([jnp.broadcast_to(rpb[..., :1], rpb.shape[:-1] + (edge,)), rpb,
                         jnp.broadcast_to(rpb[..., -1:], rpb.shape[:-1] + (edge + 1,))], axis=-1).astype(F32)
    table = jnp.where(col_ok, _toeplitz(f, GRID_W, GRID_W, GRID_W - 1) * LOG2E, NEG_INF)
    slabs = jnp.stack([table[:, d0:d0 + NA_ROWS] for d0 in range(NA_ROWS)], axis=1)
    return slabs.transpose(0, 1, 3, 2, 4).reshape(h, NA_ROWS, GRID_W, NA_ROWS * GRID_W)


def _na_attention(qkv, slabs, idx, *, batch, seq):
    rows = seq // GRID_W
    assert rows >= NA_ROWS and seq % GRID_W == 0
    blk = (None, seq, HEAD_DIM)
    return pl.pallas_call(
        functools.partial(_na_kernel, rows=rows),
        out_shape=jax.ShapeDtypeStruct((batch, seq, NA_HEADS * HEAD_DIM), BF16),
        grid=(batch, NA_HEADS),
        in_specs=[
            pl.BlockSpec(blk, lambda b, h: (b, 0, h)),
            pl.BlockSpec(blk, lambda b, h: (b, 0, NA_HEADS + h)),
            pl.BlockSpec(blk, lambda b, h: (b, 0, 2 * NA_HEADS + h)),
            pl.BlockSpec((None, None, NA_ROWS, GRID_W, NA_ROWS * GRID_W), lambda b, h: (idx, h, 0, 0, 0)),
        ],
        out_specs=pl.BlockSpec(blk, lambda b, h: (b, 0, h)),
        compiler_params=_params("parallel", "parallel"),
        name="na_attention",
    )(qkv, qkv, qkv, slabs)


def _dil_kernel(*refs, seq):
    qkv_refs = refs[:9]
    bias_ref, o_ref = refs[9], refs[10]
    x32, qs, kp, vp = refs[11:15]
    otok = refs[15:18]
    ltok = refs[18:21]

    for g, (_, d) in enumerate(DIL_CONFIG):
        q_ref, k_ref, v_ref = qkv_refs[3 * g:3 * g + 3]
        sub_len = seq // d
        nb = sub_len // DIL_QB
        seg = sub_len + 2 * DIL_RADIUS
        assert nb * DIL_QB == sub_len and nb & (nb - 1) == 0

        for dst in (kp, vp):
            dst[0:DIL_RADIUS, :] = jnp.zeros((DIL_RADIUS, HEAD_DIM), BF16)
            for res in range(d):
                hi = res * seg + DIL_RADIUS + sub_len
                dst[hi:hi + 2 * DIL_RADIUS, :] = jnp.zeros((2 * DIL_RADIUS, HEAD_DIM), BF16)
        if d == 1:
            qs[...] = q_ref[...]
            kp[DIL_RADIUS:DIL_RADIUS + seq, :] = k_ref[...]
            vp[DIL_RADIUS:DIL_RADIUS + seq, :] = v_ref[...]
        else:
            x32[...] = q_ref[...].astype(F32)
            for res in range(d):
                qs[res * sub_len:(res + 1) * sub_len, :] = x32[pl.ds(res, sub_len, stride=d), :].astype(BF16)
            for src, dst in ((k_ref, kp), (v_ref, vp)):
                x32[...] = src[...].astype(F32)
                for res in range(d):
                    lo = res * seg + DIL_RADIUS
                    dst[lo:lo + sub_len, :] = x32[pl.ds(res, sub_len, stride=d), :].astype(BF16)

        def body(it, carry, g=g, d=d, nb=nb):
            nn = [it * ATTN_UNROLL + u for u in range(ATTN_UNROLL)]
            res = [lax.shift_right_logical(n, nb.bit_length() - 1) for n in nn]
            blk = [n - r * nb for n, r in zip(nn, res)]
            qoff = [pl.multiple_of(n * DIL_QB, DIL_QB) for n in nn]
            koff = [pl.multiple_of((n + r) * DIL_QB, DIL_QB) for n, r in zip(nn, res)]
            s = [lax.dot_general(qs[pl.ds(qoff[u], DIL_QB), :], kp[pl.ds(koff[u], DIL_KB), :],
                                 (((1,), (1,)), ((), ())), preferred_element_type=F32)
                 for u in range(ATTN_UNROLL)]
            p, den, lse = [], [], []
            for u in range(ATTN_UNROLL):
                edge = jnp.where(blk[u] == 0, 1, 0) + jnp.where(blk[u] == nb - 1, 2, 0)
                su = s[u] * SCALE + bias_ref[g, edge]
                m = jnp.max(su, axis=-1, keepdims=True)
                pu = jnp.exp(su - m)
                du = jnp.sum(pu, axis=-1, keepdims=True)
                p.append(pu.astype(BF16))
                den.append(du)
                lse.append(m + jnp.log(du))
            o = [jnp.dot(p[u], vp[pl.ds(koff[u], DIL_KB), :], preferred_element_type=F32)
                 for u in range(ATTN_UNROLL)]
            for u in range(ATTN_UNROLL):
                if d == 1:
                    rows = pl.ds(qoff[u], DIL_QB)
                else:
                    rows = pl.ds(res[u] + blk[u] * (DIL_QB * d), DIL_QB, stride=d)
                otok[g][rows, :] = o[u] / den[u]
                ltok[g][rows, :] = jnp.broadcast_to(lse[u], (DIL_QB, HEAD_DIM))
            return carry

        assert (seq // DIL_QB) % ATTN_UNROLL == 0
        lax.fori_loop(0, seq // (DIL_QB * ATTN_UNROLL), body, 0)

    chunk = 256

    def merge(c, carry):
        rows = pl.ds(pl.multiple_of(c * chunk, chunk), chunk)
        l0, l1, l2 = ltok[0][rows, :], ltok[1][rows, :], ltok[2][rows, :]
        m = jnp.maximum(jnp.maximum(l0, l1), l2)
        e0, e1, e2 = jnp.exp(l0 - m), jnp.exp(l1 - m), jnp.exp(l2 - m)
        num = e0 * otok[0][rows, :] + e1 * otok[1][rows, :] + e2 * otok[2][rows, :]
        o_ref[rows, :] = (num / (e0 + e1 + e2)).astype(o_ref.dtype)
        return carry

    lax.fori_loop(0, seq // chunk, merge, 0)


def _t5_bucket(rel):
    half = N_BUCKETS // 2
    max_exact = half // 2
    sign = jnp.where(rel > 0, half, 0)
    n = jnp.abs(rel)
    nf = jnp.maximum(n, 1).astype(F32)
    large = max_exact + (jnp.log(nf / max_exact) / math.log(MAX_DISTANCE / max_exact)
                         * (half - max_exact)).astype(jnp.int32)
    large = jnp.minimum(large, half - 1)
    return sign + jnp.where(n < max_exact, n, large)


def _dil_bias_tiles(rel_bias):
    rel = jnp.arange(-DIL_RADIUS, DIL_RADIUS + 1)
    band = []
    for g, (window, d) in enumerate(DIL_CONFIG):
        assert window // (2 * d) == DIL_RADIUS
        band.append(rel_bias[_t5_bucket(rel * d)][:, g * DIL_HEADS:(g + 1) * DIL_HEADS].T)
    band = jnp.stack(band, axis=0).astype(F32)
    lead = band.shape[:-1]
    f = jnp.concatenate([jnp.full(lead + (DIL_QB - 1,), NEG_INF, F32), band,
                         jnp.full(lead + (DIL_QB,), NEG_INF, F32)], axis=-1)
    tile = _toeplitz(f, DIL_QB, DIL_KB, DIL_QB - 1)
    kj = np.arange(DIL_KB)
    off_seq = np.stack([((e & 1) != 0) & (kj < DIL_RADIUS) | ((e & 2) != 0) & (kj >= DIL_QB + DIL_RADIUS)
                        for e in range(4)])
    return jnp.where(off_seq[None, :, None, None, :], NEG_INF, tile[:, None])


def _dil_attention(qkv, tiles, *, batch, seq):
    max_d = max(d for _, d in DIL_CONFIG)
    assert seq % (max_d * DIL_QB) == 0
    blk = (None, seq, HEAD_DIM)

    def col(g, s):
        return lambda b, h: (b, 0, (3 * g + s) * DIL_HEADS + h)

    in_specs = [pl.BlockSpec(blk, col(g, s)) for g in range(N_GROUPS) for s in range(3)]
    in_specs.append(pl.BlockSpec((N_GROUPS, 4, None, DIL_QB, DIL_KB), lambda b, h: (0, 0, h, 0, 0)))
    pad_rows = seq + 2 * DIL_RADIUS * max_d + DIL_RADIUS
    scratch = [
        pltpu.VMEM((seq, HEAD_DIM), F32),
        pltpu.VMEM((seq, HEAD_DIM), BF16),
        pltpu.VMEM((pad_rows, HEAD_DIM), BF16),
        pltpu.VMEM((pad_rows, HEAD_DIM), BF16),
    ]
    scratch += [pltpu.VMEM((seq, HEAD_DIM), F32) for _ in range(2 * N_GROUPS)]
    return pl.pallas_call(
        functools.partial(_dil_kernel, seq=seq),
        out_shape=jax.ShapeDtypeStruct((batch, seq, DIL_HEADS * HEAD_DIM), BF16),
        grid=(batch, DIL_HEADS),
        in_specs=in_specs,
        out_specs=pl.BlockSpec(blk, lambda b, h: (b, 0, h)),
        scratch_shapes=scratch,
        compiler_params=_params("parallel", "parallel"),
        name="dil_attention",
    )(*([qkv] * 9), tiles)


def _trunk(x, p):
    batch, seq, _ = x.shape
    x = x.reshape(batch * seq, D_MODEL)
    for i in range(DEPTH):
        j = i // N_MIXERS
        x = _ffn(x, p, i, 0)
        if i % N_MIXERS == 0:
            qkv = _proj(x, p["na_qkv"], j).reshape(batch, seq, -1)
            a = _na_attention(qkv, p["na_slabs"], j, batch=batch, seq=seq)
            w_o = p["na_o"]
        else:
            qkv = _proj(x, p["dil_qkv"], j).reshape(batch, seq, -1)
            a = _dil_attention(qkv, p["dil_tiles"], batch=batch, seq=seq)
            w_o = p["dil_o"]
        x = _oproj(a.reshape(batch * seq, -1), w_o, j, x, p, i)
        x = _ffn(x, p, i, 1)
    return x.reshape(batch, seq, D_MODEL)


def kernel(x_prompt, x_sample, ln_g, ln_b, ffn_w_gate, ffn_w_up, ffn_w_down, na_w_qkv, na_w_o, na_rpb, dil_w_qkv, dil_w_o, rel_bias):
    p = {
        "ln_g": ln_g.astype(F32).reshape(DEPTH, 3, 1, D_MODEL),
        "ln_b": ln_b.astype(F32).reshape(DEPTH, 3, 1, D_MODEL),
        "wg": ffn_w_gate.astype(BF16),
        "wu": ffn_w_up.astype(BF16),
        "wd": ffn_w_down.astype(BF16),
        "na_qkv": na_w_qkv.astype(BF16),
        "na_o": na_w_o.astype(BF16),
        "na_slabs": jax.vmap(_na_bias_slabs)(na_rpb),
        "dil_qkv": dil_w_qkv.astype(BF16),
        "dil_o": dil_w_o.astype(BF16),
        "dil_tiles": _dil_bias_tiles(rel_bias),
    }
    return (_trunk(x_prompt, p), _trunk(x_sample, p))
```

```python
import functools
import math

import jax
import jax.numpy as jnp
import numpy as np
from jax import lax
from jax.experimental import pallas as pl
from jax.experimental.pallas import tpu as pltpu

D_MODEL = 2048
DEPTH = 4
GRID_W = 64
HEAD_DIM = 128
N_MIXERS = 2
NA_HEADS = D_MODEL // HEAD_DIM
NA_ROWS = 8
NA_COLS = 16
DIL_CONFIG = ((128, 1), (512, 4), (2048, 16))
N_GROUPS = len(DIL_CONFIG)
DIL_HEADS = D_MODEL // (2 * HEAD_DIM)
DIL_RADIUS = 64
N_BUCKETS = 32
MAX_DISTANCE = 1024
D_FF = 5632
ALPHA = (2 * DEPTH) ** 0.25
LN_EPS = 1e-5
NEG_INF = -1e30
SCALE = HEAD_DIM ** -0.5
LOG2E = math.log2(math.e)

VMEM_LIMIT_BYTES = 56 * 1024 * 1024

BF16 = jnp.bfloat16
F32 = jnp.float32

FFN_ROW_BLOCK = 256
OPROJ_ROW_BLOCK = 128

DIL_QB = 128
DIL_KB = DIL_QB + 2 * DIL_RADIUS
DIL_RES_PER_BLOCK = 4

ATTN_UNROLL = 8
NA_UNROLL = 32
NA_LOOKAHEAD = 16


def _layer_norm(y, g, b, eps=LN_EPS):
    mu = jnp.mean(y, axis=-1, keepdims=True)
    yc = y - mu
    var = jnp.mean(yc * yc, axis=-1, keepdims=True)
    return yc * lax.rsqrt(var + eps) * g + b


def _params(*semantics):
    return pltpu.CompilerParams(dimension_semantics=semantics, vmem_limit_bytes=VMEM_LIMIT_BYTES)


def _ffn_kernel(x_hbm, wg_ref, wu_ref, wd_ref, g_ref, b_ref, o_ref, x_stage, xb_ref, x_sem, *, tm):
    i = pl.program_id(0)
    k = pl.program_id(1)
    last_k = pl.num_programs(1) - 1

    def x_copy(tile):
        return pltpu.make_async_copy(x_hbm.at[pl.ds(tile * tm, tm), :], x_stage, x_sem)

    def step(first, last):
        row_block = FFN_ROW_BLOCK if first or last else tm
        for r in range(tm // row_block):
            rows = pl.ds(r * row_block, row_block)
            if first:
                x = x_stage[rows, :]
                xb = x.astype(BF16)
                xb_ref[rows, :] = xb
                z = (2.0 * ALPHA) * x
            else:
                xb = xb_ref[rows, :]
                z = o_ref[rows, :]
            gate = jnp.dot(xb, wg_ref[...], preferred_element_type=F32)
            up = jnp.dot(xb, wu_ref[...], preferred_element_type=F32)
            h = (gate * (1.0 / (1.0 + jnp.exp(-gate))) * up).astype(BF16)
            z = z + jnp.dot(h, wd_ref[...], preferred_element_type=F32)
            if last:
                z = _layer_norm(z, g_ref[...], b_ref[...], eps=4.0 * LN_EPS)
            o_ref[rows, :] = z

    @pl.when(k == 0)
    def _first_step():
        @pl.when(i == 0)
        def _first_tile():
            x_copy(0).start()

        x_copy(i).wait()
        step(True, False)

    @pl.when((k == 1) & (i + 1 < pl.num_programs(0)))
    def _prefetch():
        x_copy(i + 1).start()

    @pl.when((k > 0) & (k < last_k))
    def _middle_step():
        step(False, False)

    @pl.when(k == last_k)
    def _last_step():
        step(False, True)


def _ln_spec(layer, idx):
    return pl.BlockSpec((None, None, 1, D_MODEL), lambda *_: (layer, idx, 0, 0))


def _ffn(x, p, layer, half, *, tm=1024, tf=512):
    m = x.shape[0]
    assert m % tm == 0 and tm % FFN_ROW_BLOCK == 0 and D_FF % tf == 0 and D_FF // tf >= 3
    return pl.pallas_call(
        functools.partial(_ffn_kernel, tm=tm),
        out_shape=jax.ShapeDtypeStruct((m, D_MODEL), F32),
        grid=(m // tm, D_FF // tf),
        in_specs=[
            pl.BlockSpec(memory_space=pl.ANY),
            pl.BlockSpec((None, None, D_MODEL, tf), lambda i, k: (layer, half, 0, k)),
            pl.BlockSpec((None, None, D_MODEL, tf), lambda i, k: (layer, half, 0, k)),
            pl.BlockSpec((None, None, tf, D_MODEL), lambda i, k: (layer, half, k, 0)),
            _ln_spec(layer, 2 * half),
            _ln_spec(layer, 2 * half),
        ],
        out_specs=pl.BlockSpec((tm, D_MODEL), lambda i, k: (i, 0)),
        scratch_shapes=[pltpu.VMEM((tm, D_MODEL), F32), pltpu.VMEM((tm, D_MODEL), BF16),
                        pltpu.SemaphoreType.DMA(())],
        compiler_params=_params("arbitrary", "arbitrary"),
        name="ffn",
    )(x, p["wg"], p["wu"], p["wd"], p["ln_g"], p["ln_b"])


def _proj_kernel(x_ref, w_ref, o_ref, xb_ref):
    @pl.when(pl.program_id(1) == 0)
    def _init():
        xb_ref[...] = x_ref[...].astype(BF16)

    acc = jnp.dot(xb_ref[...], w_ref[...], preferred_element_type=F32).astype(o_ref.dtype)
    for c in range(o_ref.shape[0]):
        o_ref[c] = acc[:, c * HEAD_DIM:(c + 1) * HEAD_DIM]


def _proj(x, w, idx, *, tm=1024, tn=1024):
    m, n = x.shape[0], w.shape[2]
    assert m % tm == 0 and n % tn == 0 and tn % HEAD_DIM == 0
    return pl.pallas_call(
        _proj_kernel,
        out_shape=jax.ShapeDtypeStruct((n // HEAD_DIM, m, HEAD_DIM), BF16),
        grid=(m // tm, n // tn),
        in_specs=[
            pl.BlockSpec((tm, D_MODEL), lambda i, j: (i, 0)),
            pl.BlockSpec((None, D_MODEL, tn), lambda i, j: (idx, 0, j)),
        ],
        out_specs=pl.BlockSpec((tn // HEAD_DIM, tm, HEAD_DIM), lambda i, j: (j, i, 0)),
        scratch_shapes=[pltpu.VMEM((tm, D_MODEL), BF16)],
        compiler_params=_params("parallel", "arbitrary"),
        name="qkv_proj",
    )(x, w)


def _oproj_kernel(a_ref, w_ref, x_ref, g_ref, b_ref, o_ref):
    for r in range(a_ref.shape[0] // OPROJ_ROW_BLOCK):
        rows = pl.ds(r * OPROJ_ROW_BLOCK, OPROJ_ROW_BLOCK)
        mix = jnp.dot(a_ref[rows, :], w_ref[...], preferred_element_type=F32)
        y = ALPHA * x_ref[rows, :] + mix
        o_ref[rows, :] = _layer_norm(y, g_ref[...], b_ref[...])


def _oproj(a, w, idx, x, p, layer, *, tm=1024):
    m, kdim = a.shape
    assert m % tm == 0 and tm % OPROJ_ROW_BLOCK == 0
    return pl.pallas_call(
        _oproj_kernel,
        out_shape=jax.ShapeDtypeStruct((m, D_MODEL), F32),
        grid=(m // tm,),
        in_specs=[
            pl.BlockSpec((tm, kdim), lambda i: (i, 0)),
            pl.BlockSpec((None, kdim, D_MODEL), lambda i: (idx, 0, 0), pipeline_mode=pl.Buffered(1)),
            pl.BlockSpec((tm, D_MODEL), lambda i: (i, 0)),
            _ln_spec(layer, 1),
            _ln_spec(layer, 1),
        ],
        out_specs=pl.BlockSpec((tm, D_MODEL), lambda i: (i, 0)),
        compiler_params=_params("parallel"),
        name="out_proj",
    )(a, w, x, p["ln_g"], p["ln_b"])


def _na_kernel(q_ref, k_ref, v_ref, bias_ref, o_ref, *, rows):
    kw = NA_ROWS * GRID_W

    def body(it, carry):
        rr = [it * NA_UNROLL + u for u in range(NA_UNROLL)]
        rs = [jnp.clip(r - NA_ROWS // 2, 0, rows - NA_ROWS) for r in rr]
        kstart = [pl.multiple_of(x * GRID_W, GRID_W) for x in rs]
        qstart = [pl.multiple_of(r * GRID_W, GRID_W) for r in rr]
        s = {}
        for step in range(NA_UNROLL + NA_LOOKAHEAD):
            if step < NA_UNROLL:
                s[step] = lax.dot_general(q_ref[pl.ds(qstart[step], GRID_W), :], k_ref[pl.ds(kstart[step], kw), :],
                                          (((1,), (1,)), ((), ())), preferred_element_type=F32)
            u = step - NA_LOOKAHEAD
            if u >= 0:
                su = s.pop(u) * (SCALE * LOG2E) + bias_ref[rs[u] - rr[u] + (NA_ROWS - 1)]
                pu = jnp.exp2(su - jnp.max(su, axis=-1, keepdims=True))
                den = jnp.sum(pu, axis=-1, keepdims=True)
                o = jnp.dot(pu.astype(BF16), v_ref[pl.ds(kstart[u], kw), :], preferred_element_type=F32)
                o_ref[pl.ds(qstart[u], GRID_W), :] = (o / den).astype(o_ref.dtype)
        return carry

    assert rows % NA_UNROLL == 0
    lax.fori_loop(0, rows // NA_UNROLL, body, 0)


def _toeplitz(f, n_rows, n_cols, offset):
    p = f.shape[-1]
    assert offset - (n_rows - 1) >= 0 and offset + n_cols <= p - 1
    lead = f.shape[:-1]
    x = jnp.broadcast_to(f[..., None, :], lead + (n_rows, p)).reshape(lead + (n_rows * p,))
    y = x[..., :n_rows * (p - 1)].reshape(lead + (n_rows, p - 1))
    return y[..., offset:offset + n_cols]


def _na_bias_slabs(rpb):
    h = rpb.shape[0]
    c = np.arange(GRID_W)
    col_start = np.clip(c - NA_COLS // 2, 0, GRID_W - NA_COLS)
    col_ok = (c[None, :] >= col_start[:, None]) & (c[None, :] < col_start[:, None] + NA_COLS)
    edge = GRID_W - NA_COLS
    f = jnp.concatenate([jnp.broadcast_to(rpb[..., :1], rpb.shape[:-1] + (edge,)), rpb,
                         jnp.broadcast_to(rpb[..., -1:], rpb.shape[:-1] + (edge + 1,))], axis=-1).astype(F32)
    table = jnp.where(col_ok, _toeplitz(f, GRID_W, GRID_W, GRID_W - 1) * LOG2E, NEG_INF)
    slabs = jnp.stack([table[:, d0:d0 + NA_ROWS] for d0 in range(NA_ROWS)], axis=1)
    return slabs.transpose(0, 1, 3, 2, 4).reshape(h, NA_ROWS, GRID_W, NA_ROWS * GRID_W)


def _na_attention(qkv, slabs, idx, *, batch, seq):
    rows = seq // GRID_W
    assert rows >= NA_ROWS and seq % GRID_W == 0
    blk = (None, seq, HEAD_DIM)
    hblk = (None, None, seq, HEAD_DIM)
    return pl.pallas_call(
        functools.partial(_na_kernel, rows=rows),
        out_shape=jax.ShapeDtypeStruct((batch, seq, NA_HEADS * HEAD_DIM), BF16),
        grid=(batch, NA_HEADS),
        in_specs=[
            pl.BlockSpec(hblk, lambda b, h: (h, b, 0, 0)),
            pl.BlockSpec(hblk, lambda b, h: (NA_HEADS + h, b, 0, 0)),
            pl.BlockSpec(hblk, lambda b, h: (2 * NA_HEADS + h, b, 0, 0)),
            pl.BlockSpec((None, None, NA_ROWS, GRID_W, NA_ROWS * GRID_W), lambda b, h: (idx, h, 0, 0, 0)),
        ],
        out_specs=pl.BlockSpec(blk, lambda b, h: (b, 0, h)),
        compiler_params=_params("parallel", "parallel"),
        name="na_attention",
    )(qkv, qkv, qkv, slabs)


def _dil_kernel(*refs, seq):
    n_in = sum(3 * max(1, d // DIL_RES_PER_BLOCK) for _, d in DIL_CONFIG)
    in_refs = refs[:n_in]
    bias_ref, o_ref = refs[n_in], refs[n_in + 1]
    qs, kp, vp = refs[n_in + 2:n_in + 5]
    otok = refs[n_in + 5:n_in + 8]
    ltok = refs[n_in + 8:n_in + 11]

    pos = 0
    for g, (_, d) in enumerate(DIL_CONFIG):
        n_blk = max(1, d // DIL_RES_PER_BLOCK)
        q_blks, k_blks, v_blks = (in_refs[pos + s * n_blk:pos + (s + 1) * n_blk] for s in range(3))
        pos += 3 * n_blk
        sub_len = seq // d
        nb = sub_len // DIL_QB
        seg = sub_len + 2 * DIL_RADIUS
        assert nb * DIL_QB == sub_len and nb & (nb - 1) == 0

        for dst in (kp, vp):
            dst[0:DIL_RADIUS, :] = jnp.zeros((DIL_RADIUS, HEAD_DIM), BF16)
            for res in range(d):
                hi = res * seg + DIL_RADIUS + sub_len
                dst[hi:hi + 2 * DIL_RADIUS, :] = jnp.zeros((2 * DIL_RADIUS, HEAD_DIM), BF16)
        per_blk = d // n_blk
        for res in range(d):
            lanes = slice((res % per_blk) * HEAD_DIM, (res % per_blk + 1) * HEAD_DIM)
            lo = res * seg + DIL_RADIUS
            qs[res * sub_len:(res + 1) * sub_len, :] = q_blks[res // per_blk][:, lanes]
            kp[lo:lo + sub_len, :] = k_blks[res // per_blk][:, lanes]
            vp[lo:lo + sub_len, :] = v_blks[res // per_blk][:, lanes]

        def body(it, carry, g=g, d=d, nb=nb):
            nn = [it * ATTN_UNROLL + u for u in range(ATTN_UNROLL)]
            res = [lax.shift_right_logical(n, nb.bit_length() - 1) for n in nn]
            blk = [n - r * nb for n, r in zip(nn, res)]
            qoff = [pl.multiple_of(n * DIL_QB, DIL_QB) for n in nn]
            koff = [pl.multiple_of((n + r) * DIL_QB, DIL_QB) for n, r in zip(nn, res)]
            s = [lax.dot_general(qs[pl.ds(qoff[u], DIL_QB), :], kp[pl.ds(koff[u], DIL_KB), :],
                                 (((1,), (1,)), ((), ())), preferred_element_type=F32)
                 for u in range(ATTN_UNROLL)]
            p, den, lse = [], [], []
            for u in range(ATTN_UNROLL):
                edge = jnp.where(blk[u] == 0, 1, 0) + jnp.where(blk[u] == nb - 1, 2, 0)
                su = s[u] * SCALE + bias_ref[g, edge]
                m = jnp.max(su, axis=-1, keepdims=True)
                pu = jnp.exp(su - m)
                du = jnp.sum(pu, axis=-1, keepdims=True)
                p.append(pu.astype(BF16))
                den.append(du)
                lse.append(m + jnp.log(du))
            o = [jnp.dot(p[u], vp[pl.ds(koff[u], DIL_KB), :], preferred_element_type=F32)
                 for u in range(ATTN_UNROLL)]
            for u in range(ATTN_UNROLL):
                if d == 1:
                    rows = pl.ds(qoff[u], DIL_QB)
                else:
                    rows = pl.ds(res[u] + blk[u] * (DIL_QB * d), DIL_QB, stride=d)
                otok[g][rows, :] = o[u] / den[u]
                ltok[g][rows, :] = jnp.broadcast_to(lse[u], (DIL_QB, HEAD_DIM))
            return carry

        assert (seq // DIL_QB) % ATTN_UNROLL == 0
        lax.fori_loop(0, seq // (DIL_QB * ATTN_UNROLL), body, 0)

    chunk = 256

    def merge(c, carry):
        rows = pl.ds(pl.multiple_of(c * chunk, chunk), chunk)
        l0, l1, l2 = ltok[0][rows, :], ltok[1][rows, :], ltok[2][rows, :]
        m = jnp.maximum(jnp.maximum(l0, l1), l2)
        e0, e1, e2 = jnp.exp(l0 - m), jnp.exp(l1 - m), jnp.exp(l2 - m)
        num = e0 * otok[0][rows, :] + e1 * otok[1][rows, :] + e2 * otok[2][rows, :]
        o_ref[rows, :] = (num / (e0 + e1 + e2)).astype(o_ref.dtype)
        return carry

    lax.fori_loop(0, seq // chunk, merge, 0)


def _t5_bucket(rel):
    half = N_BUCKETS // 2
    max_exact = half // 2
    sign = jnp.where(rel > 0, half, 0)
    n = jnp.abs(rel)
    nf = jnp.maximum(n, 1).astype(F32)
    large = max_exact + (jnp.log(nf / max_exact) / math.log(MAX_DISTANCE / max_exact)
                         * (half - max_exact)).astype(jnp.int32)
    large = jnp.minimum(large, half - 1)
    return sign + jnp.where(n < max_exact, n, large)


def _dil_bias_tiles(rel_bias):
    rel = jnp.arange(-DIL_RADIUS, DIL_RADIUS + 1)
    band = []
    for g, (window, d) in enumerate(DIL_CONFIG):
        assert window // (2 * d) == DIL_RADIUS
        band.append(rel_bias[_t5_bucket(rel * d)][:, g * DIL_HEADS:(g + 1) * DIL_HEADS].T)
    band = jnp.stack(band, axis=0).astype(F32)
    lead = band.shape[:-1]
    f = jnp.concatenate([jnp.full(lead + (DIL_QB - 1,), NEG_INF, F32), band,
                         jnp.full(lead + (DIL_QB,), NEG_INF, F32)], axis=-1)
    tile = _toeplitz(f, DIL_QB, DIL_KB, DIL_QB - 1)
    kj = np.arange(DIL_KB)
    off_seq = np.stack([((e & 1) != 0) & (kj < DIL_RADIUS) | ((e & 2) != 0) & (kj >= DIL_QB + DIL_RADIUS)
                        for e in range(4)])
    return jnp.where(off_seq[None, :, None, None, :], NEG_INF, tile[:, None])


def _dil_attention(qkv, tiles, *, batch, seq):
    max_d = max(d for _, d in DIL_CONFIG)
    assert seq % (max_d * DIL_QB) == 0
    blk = (None, seq, HEAD_DIM)

    in_specs, operands = [], []
    for g, (_, d) in enumerate(DIL_CONFIG):
        n_blk = max(1, d // DIL_RES_PER_BLOCK)
        width = (d // n_blk) * HEAD_DIM
        view = qkv.reshape(qkv.shape[0], batch, seq // d, d * HEAD_DIM)
        for s in range(3):
            for rb in range(n_blk):
                in_specs.append(pl.BlockSpec((None, None, seq // d, width),
                                             lambda b, h, c=(3 * g + s) * DIL_HEADS, rb=rb: (c + h, b, 0, rb)))
                operands.append(view)
    in_specs.append(pl.BlockSpec((N_GROUPS, 4, None, DIL_QB, DIL_KB), lambda b, h: (0, 0, h, 0, 0)))
    pad_rows = seq + 2 * DIL_RADIUS * max_d + DIL_RADIUS
    scratch = [
        pltpu.VMEM((seq, HEAD_DIM), BF16),
        pltpu.VMEM((pad_rows, HEAD_DIM), BF16),
        pltpu.VMEM((pad_rows, HEAD_DIM), BF16),
    ]
    scratch += [pltpu.VMEM((seq, HEAD_DIM), F32) for _ in range(2 * N_GROUPS)]
    return pl.pallas_call(
        functools.partial(_dil_kernel, seq=seq),
        out_shape=jax.ShapeDtypeStruct((batch, seq, DIL_HEADS * HEAD_DIM), BF16),
        grid=(batch, DIL_HEADS),
        in_specs=in_specs,
        out_specs=pl.BlockSpec(blk, lambda b, h: (b, 0, h)),
        scratch_shapes=scratch,
        compiler_params=_params("parallel", "parallel"),
        name="dil_attention",
    )(*operands, tiles)


def _trunk(x, p):
    batch, seq, _ = x.shape
    x = x.reshape(batch * seq, D_MODEL)
    for i in range(DEPTH):
        j = i // N_MIXERS
        x = _ffn(x, p, i, 0)
        if i % N_MIXERS == 0:
            qkv = _proj(x, p["na_qkv"], j).reshape(-1, batch, seq, HEAD_DIM)
            a = _na_attention(qkv, p["na_slabs"], j, batch=batch, seq=seq)
            w_o = p["na_o"]
        else:
            qkv = _proj(x, p["dil_qkv"], j).reshape(-1, batch, seq, HEAD_DIM)
            a = _dil_attention(qkv, p["dil_tiles"], batch=batch, seq=seq)
            w_o = p["dil_o"]
        x = _oproj(a.reshape(batch * seq, -1), w_o, j, x, p, i)
        x = _ffn(x, p, i, 1)
    return x.reshape(batch, seq, D_MODEL)


def kernel(x_prompt, x_sample, ln_g, ln_b, ffn_w_gate, ffn_w_up, ffn_w_down, na_w_qkv, na_w_o, na_rpb, dil_w_qkv, dil_w_o, rel_bias):
    p = {
        "ln_g": ln_g.astype(F32).reshape(DEPTH, 3, 1, D_MODEL),
        "ln_b": ln_b.astype(F32).reshape(DEPTH, 3, 1, D_MODEL),
        "wg": ffn_w_gate.astype(BF16),
        "wu": ffn_w_up.astype(BF16),
        "wd": ffn_w_down.astype(BF16),
        "na_qkv": na_w_qkv.astype(BF16),
        "na_o": na_w_o.astype(BF16),
        "na_slabs": jax.vmap(_na_bias_slabs)(na_rpb),
        "dil_qkv": dil_w_qkv.astype(BF16),
        "dil_o": dil_w_o.astype(BF16),
        "dil_tiles": _dil_bias_tiles(rel_bias),
    }
    return (_trunk(x_prompt, p), _trunk(x_sample, p))
```

```python
import functools
import math

import jax
import jax.numpy as jnp
import numpy as np
from jax import lax
from jax.experimental import pallas as pl
from jax.experimental.pallas import tpu as pltpu

D_MODEL = 2048
DEPTH = 4
GRID_W = 64
HEAD_DIM = 128
N_MIXERS = 2
NA_HEADS = D_MODEL // HEAD_DIM
NA_ROWS = 8
NA_COLS = 16
DIL_CONFIG = ((128, 1), (512, 4), (2048, 16))
N_GROUPS = len(DIL_CONFIG)
DIL_HEADS = D_MODEL // (2 * HEAD_DIM)
DIL_RADIUS = 64
N_BUCKETS = 32
MAX_DISTANCE = 1024
D_FF = 5632
ALPHA = (2 * DEPTH) ** 0.25
LN_EPS = 1e-5
NEG_INF = -1e30
SCALE = HEAD_DIM ** -0.5
LOG2E = math.log2(math.e)

VMEM_LIMIT_BYTES = 56 * 1024 * 1024

BF16 = jnp.bfloat16
F32 = jnp.float32

FFN_ROW_BLOCK = 256
OPROJ_ROW_BLOCK = 128

DIL_QB = 128
DIL_KB = DIL_QB + 2 * DIL_RADIUS

ATTN_UNROLL = 8
NA_UNROLL = 32
NA_LOOKAHEAD = 16


def _layer_norm(y, g, b, eps=LN_EPS):
    mu = jnp.mean(y, axis=-1, keepdims=True)
    yc = y - mu
    var = jnp.mean(yc * yc, axis=-1, keepdims=True)
    return yc * lax.rsqrt(var + eps) * g + b


def _params(*semantics):
    return pltpu.CompilerParams(dimension_semantics=semantics, vmem_limit_bytes=VMEM_LIMIT_BYTES)


def _ffn_kernel(x_hbm, wg_hbm, wu_hbm, wd_hbm, g_ref, b_ref, o_ref,
                x_stage, xb_ref, wg_buf, wu_buf, wd_buf, x_sem, w_sem, *, tm, tf, layer, half):
    i = pl.program_id(0)
    n_tiles = pl.num_programs(0)
    n_chunks = D_FF // tf

    def x_copy(tile):
        return pltpu.make_async_copy(x_hbm.at[pl.ds(tile * tm, tm), :], x_stage, x_sem)

    def w_copies(k, slot):
        cols = pl.ds(pl.multiple_of(k * tf, tf), tf)
        return (pltpu.make_async_copy(wg_hbm.at[layer, half, :, cols], wg_buf.at[slot], w_sem.at[0, slot]),
                pltpu.make_async_copy(wu_hbm.at[layer, half, :, cols], wu_buf.at[slot], w_sem.at[1, slot]),
                pltpu.make_async_copy(wd_hbm.at[layer, half, cols, :], wd_buf.at[slot], w_sem.at[2, slot]))

    def start_w(k, slot):
        for c in w_copies(k, slot):
            c.start()

    def wait_w(k, slot):
        for c in w_copies(k, slot):
            c.wait()

    def chunk(slot, first, last):
        row_block = FFN_ROW_BLOCK if first or last else tm
        for r in range(tm // row_block):
            rows = pl.ds(r * row_block, row_block)
            if first:
                x = x_stage[rows, :]
                xb = x.astype(BF16)
                xb_ref[rows, :] = xb
                z = (2.0 * ALPHA) * x
            else:
                xb = xb_ref[rows, :]
                z = o_ref[rows, :]
            gate = jnp.dot(xb, wg_buf[slot], preferred_element_type=F32)
            up = jnp.dot(xb, wu_buf[slot], preferred_element_type=F32)
            h = (gate * (1.0 / (1.0 + jnp.exp(-gate))) * up).astype(BF16)
            z = z + jnp.dot(h, wd_buf[slot], preferred_element_type=F32)
            if last:
                z = _layer_norm(z, g_ref[...], b_ref[...], eps=4.0 * LN_EPS)
            o_ref[rows, :] = z

    def slot_of(k):
        return lax.rem(i * n_chunks + k, 2)

    @pl.when(i == 0)
    def _prime():
        x_copy(0).start()
        start_w(0, 0)

    slot = slot_of(0)
    x_copy(i).wait()
    wait_w(0, slot)
    start_w(1, 1 - slot)
    chunk(slot, True, False)

    @pl.when(i + 1 < n_tiles)
    def _next_x():
        x_copy(i + 1).start()

    def middle(k, carry):
        slot = slot_of(k)
        wait_w(k, slot)
        start_w(k + 1, 1 - slot)
        chunk(slot, False, False)
        return carry

    lax.fori_loop(1, n_chunks - 1, middle, 0)

    slot = slot_of(n_chunks - 1)
    wait_w(n_chunks - 1, slot)

    @pl.when(i + 1 < n_tiles)
    def _next_w():
        start_w(0, 1 - slot)

    chunk(slot, False, True)


def _ln_spec(layer, idx):
    return pl.BlockSpec((None, None, 1, D_MODEL), lambda *_: (layer, idx, 0, 0))


def _ffn(x, p, layer, half, *, tm=1024, tf=512):
    m = x.shape[0]
    assert m % tm == 0 and tm % FFN_ROW_BLOCK == 0 and D_FF % tf == 0 and D_FF // tf >= 3
    return pl.pallas_call(
        functools.partial(_ffn_kernel, tm=tm, tf=tf, layer=layer, half=half),
        out_shape=jax.ShapeDtypeStruct((m, D_MODEL), F32),
        grid=(m // tm,),
        in_specs=[
            pl.BlockSpec(memory_space=pl.ANY),
            pl.BlockSpec(memory_space=pl.ANY),
            pl.BlockSpec(memory_space=pl.ANY),
            pl.BlockSpec(memory_space=pl.ANY),
            _ln_spec(layer, 2 * half),
            _ln_spec(layer, 2 * half),
        ],
        out_specs=pl.BlockSpec((tm, D_MODEL), lambda i: (i, 0)),
        scratch_shapes=[
            pltpu.VMEM((tm, D_MODEL), F32),
            pltpu.VMEM((tm, D_MODEL), BF16),
            pltpu.VMEM((2, D_MODEL, tf), BF16),
            pltpu.VMEM((2, D_MODEL, tf), BF16),
            pltpu.VMEM((2, tf, D_MODEL), BF16),
            pltpu.SemaphoreType.DMA(()),
            pltpu.SemaphoreType.DMA((3, 2)),
        ],
        compiler_params=_params("arbitrary"),
        name="ffn",
    )(x, p["wg"], p["wu"], p["wd"], p["ln_g"], p["ln_b"])


def _proj_kernel(x_ref, w_ref, o_ref, xb_ref):
    @pl.when(pl.program_id(1) == 0)
    def _init():
        xb_ref[...] = x_ref[...].astype(BF16)

    o_ref[...] = jnp.dot(xb_ref[...], w_ref[...], preferred_element_type=F32).astype(o_ref.dtype)


def _proj(x, w, idx, *, tm=1024, tn=1024):
    m, n = x.shape[0], w.shape[2]
    assert m % tm == 0 and n % tn == 0
    return pl.pallas_call(
        _proj_kernel,
        out_shape=jax.ShapeDtypeStruct((m, n), BF16),
        grid=(m // tm, n // tn),
        in_specs=[
            pl.BlockSpec((tm, D_MODEL), lambda i, j: (i, 0)),
            pl.BlockSpec((None, D_MODEL, tn), lambda i, j: (idx, 0, j)),
        ],
        out_specs=pl.BlockSpec((tm, tn), lambda i, j: (i, j)),
        scratch_shapes=[pltpu.VMEM((tm, D_MODEL), BF16)],
        compiler_params=_params("parallel", "arbitrary"),
        name="qkv_proj",
    )(x, w)


def _oproj_kernel(a_ref, w_ref, x_ref, g_ref, b_ref, o_ref):
    for r in range(a_ref.shape[0] // OPROJ_ROW_BLOCK):
        rows = pl.ds(r * OPROJ_ROW_BLOCK, OPROJ_ROW_BLOCK)
        mix = jnp.dot(a_ref[rows, :], w_ref[...], preferred_element_type=F32)
        y = ALPHA * x_ref[rows, :] + mix
        o_ref[rows, :] = _layer_norm(y, g_ref[...], b_ref[...])


def _oproj(a, w, idx, x, p, layer, *, tm=512):
    m, kdim = a.shape
    assert m % tm == 0 and tm % OPROJ_ROW_BLOCK == 0
    return pl.pallas_call(
        _oproj_kernel,
        out_shape=jax.ShapeDtypeStruct((m, D_MODEL), F32),
        grid=(m // tm,),
        in_specs=[
            pl.BlockSpec((tm, kdim), lambda i: (i, 0)),
            pl.BlockSpec((None, kdim, D_MODEL), lambda i: (idx, 0, 0)),
            pl.BlockSpec((tm, D_MODEL), lambda i: (i, 0)),
            _ln_spec(layer, 1),
            _ln_spec(layer, 1),
        ],
        out_specs=pl.BlockSpec((tm, D_MODEL), lambda i: (i, 0)),
        compiler_params=_params("parallel"),
        name="out_proj",
    )(a, w, x, p["ln_g"], p["ln_b"])


def _na_kernel(q_ref, k_ref, v_ref, bias_ref, o_ref, *, rows):
    kw = NA_ROWS * GRID_W

    def body(it, carry):
        rr = [it * NA_UNROLL + u for u in range(NA_UNROLL)]
        rs = [jnp.clip(r - NA_ROWS // 2, 0, rows - NA_ROWS) for r in rr]
        kstart = [pl.multiple_of(x * GRID_W, GRID_W) for x in rs]
        qstart = [pl.multiple_of(r * GRID_W, GRID_W) for r in rr]
        s = {}
        for step in range(NA_UNROLL + NA_LOOKAHEAD):
            if step < NA_UNROLL:
                s[step] = lax.dot_general(q_ref[pl.ds(qstart[step], GRID_W), :], k_ref[pl.ds(kstart[step], kw), :],
                                          (((1,), (1,)), ((), ())), preferred_element_type=F32)
            u = step - NA_LOOKAHEAD
            if u >= 0:
                su = s.pop(u) * (SCALE * LOG2E) + bias_ref[rs[u] - rr[u] + (NA_ROWS - 1)]
                pu = jnp.exp2(su - jnp.max(su, axis=-1, keepdims=True))
                den = jnp.sum(pu, axis=-1, keepdims=True)
                o = jnp.dot(pu.astype(BF16), v_ref[pl.ds(kstart[u], kw), :], preferred_element_type=F32)
                o_ref[pl.ds(qstart[u], GRID_W), :] = (o / den).astype(o_ref.dtype)
        return carry

    assert rows % NA_UNROLL == 0
    lax.fori_loop(0, rows // NA_UNROLL, body, 0)


def _toeplitz(f, n_rows, n_cols, offset):
    p = f.shape[-1]
    assert offset - (n_rows - 1) >= 0 and offset + n_cols <= p - 1
    lead = f.shape[:-1]
    x = jnp.broadcast_to(f[..., None, :], lead + (n_rows, p)).reshape(lead + (n_rows * p,))
    y = x[..., :n_rows * (p - 1)].reshape(lead + (n_rows, p - 1))
    return y[..., offset:offset + n_cols]


def _na_bias_slabs(rpb):
    h = rpb.shape[0]
    c = np.arange(GRID_W)
    col_start = np.clip(c - NA_COLS // 2, 0, GRID_W - NA_COLS)
    col_ok = (c[None, :] >= col_start[:, None]) & (c[None, :] < col_start[:, None] + NA_COLS)
    edge = GRID_W - NA_COLS
    f = jnp.concatenate([jnp.broadcast_to(rpb[..., :1], rpb.shape[:-1] + (edge,)), rpb,
                         jnp.broadcast_to(rpb[..., -1:], rpb.shape[:-1] + (edge + 1,))], axis=-1).astype(F32)
    table = jnp.where(col_ok, _toeplitz(f, GRID_W, GRID_W, GRID_W - 1) * LOG2E, NEG_INF)
    slabs = jnp.stack([table[:, d0:d0 + NA_ROWS] for d0 in range(NA_ROWS)], axis=1)
    return slabs.transpose(0, 1, 3, 2, 4).reshape(h, NA_ROWS, GRID_W, NA_ROWS * GRID_W)


def _na_attention(qkv, slabs, idx, *, batch, seq):
    rows = seq // GRID_W
    assert rows >= NA_ROWS and seq % GRID_W == 0
    blk = (None, seq, HEAD_DIM)
    return pl.pallas_call(
        functools.partial(_na_kernel, rows=rows),
        out_shape=jax.ShapeDtypeStruct((batch, seq, NA_HEADS * HEAD_DIM), BF16),
        grid=(batch, NA_HEADS),
        in_specs=[
            pl.BlockSpec(blk, lambda b, h: (b, 0, h)),
            pl.BlockSpec(blk, lambda b, h: (b, 0, NA_HEADS + h)),
            pl.BlockSpec(blk, lambda b, h: (b, 0, 2 * NA_HEADS + h)),
            pl.BlockSpec((None, None, NA_ROWS, GRID_W, NA_ROWS * GRID_W), lambda b, h: (idx, h, 0, 0, 0)),
        ],
        out_specs=pl.BlockSpec(blk, lambda b, h: (b, 0, h)),
        compiler_params=_params("parallel", "parallel"),
        name="na_attention",
    )(qkv, qkv, qkv, slabs)


def _dil_kernel(*refs, seq):
    qkv_refs = refs[:9]
    bias_ref, o_ref = refs[9], refs[10]
    x32, qs, kp, vp = refs[11:15]
    otok = refs[15:18]
    ltok = refs[18:21]

    for g, (_, d) in enumerate(DIL_CONFIG):
        q_ref, k_ref, v_ref = qkv_refs[3 * g:3 * g + 3]
        sub_len = seq // d
        nb = sub_len // DIL_QB
        seg = sub_len + 2 * DIL_RADIUS
        assert nb * DIL_QB == sub_len and nb & (nb - 1) == 0

        for dst in (kp, vp):
            dst[0:DIL_RADIUS, :] = jnp.zeros((DIL_RADIUS, HEAD_DIM), BF16)
            for res in range(d):
                hi = res * seg + DIL_RADIUS + sub_len
                dst[hi:hi + 2 * DIL_RADIUS, :] = jnp.zeros((2 * DIL_RADIUS, HEAD_DIM), BF16)
        if d == 1:
            qs[...] = q_ref[...]
            kp[DIL_RADIUS:DIL_RADIUS + seq, :] = k_ref[...]
            vp[DIL_RADIUS:DIL_RADIUS + seq, :] = v_ref[...]
        else:
            x32[...] = q_ref[...].astype(F32)
            for res in range(d):
                qs[res * sub_len:(res + 1) * sub_len, :] = x32[pl.ds(res, sub_len, stride=d), :].astype(BF16)
            for src, dst in ((k_ref, kp), (v_ref, vp)):
                x32[...] = src[...].astype(F32)
                for res in range(d):
                    lo = res * seg + DIL_RADIUS
                    dst[lo:lo + sub_len, :] = x32[pl.ds(res, sub_len, stride=d), :].astype(BF16)

        def body(it, carry, g=g, d=d, nb=nb):
            nn = [it * ATTN_UNROLL + u for u in range(ATTN_UNROLL)]
            res = [lax.shift_right_logical(n, nb.bit_length() - 1) for n in nn]
            blk = [n - r * nb for n, r in zip(nn, res)]
            qoff = [pl.multiple_of(n * DIL_QB, DIL_QB) for n in nn]
            koff = [pl.multiple_of((n + r) * DIL_QB, DIL_QB) for n, r in zip(nn, res)]
            s = [lax.dot_general(qs[pl.ds(qoff[u], DIL_QB), :], kp[pl.ds(koff[u], DIL_KB), :],
                                 (((1,), (1,)), ((), ())), preferred_element_type=F32)
                 for u in range(ATTN_UNROLL)]
            p, den, lse = [], [], []
            for u in range(ATTN_UNROLL):
                edge = jnp.where(blk[u] == 0, 1, 0) + jnp.where(blk[u] == nb - 1, 2, 0)
                su = s[u] * SCALE + bias_ref[g, edge]
                m = jnp.max(su, axis=-1, keepdims=True)
                pu = jnp.exp(su - m)
                du = jnp.sum(pu, axis=-1, keepdims=True)
                p.append(pu.astype(BF16))
                den.append(du)
                lse.append(m + jnp.log(du))
            o = [jnp.dot(p[u], vp[pl.ds(koff[u], DIL_KB), :], preferred_element_type=F32)
                 for u in range(ATTN_UNROLL)]
            for u in range(ATTN_UNROLL):
                if d == 1:
                    rows = pl.ds(qoff[u], DIL_QB)
                else:
                    rows = pl.ds(res[u] + blk[u] * (DIL_QB * d), DIL_QB, stride=d)
                otok[g][rows, :] = o[u] / den[u]
                ltok[g][rows, :] = jnp.broadcast_to(lse[u], (DIL_QB, HEAD_DIM))
            return carry

        assert (seq // DIL_QB) % ATTN_UNROLL == 0
        lax.fori_loop(0, seq // (DIL_QB * ATTN_UNROLL), body, 0)

    chunk = 256

    def merge(c, carry):
        rows = pl.ds(pl.multiple_of(c * chunk, chunk), chunk)
        l0, l1, l2 = ltok[0][rows, :], ltok[1][rows, :], ltok[2][rows, :]
        m = jnp.maximum(jnp.maximum(l0, l1), l2)
        e0, e1, e2 = jnp.exp(l0 - m), jnp.exp(l1 - m), jnp.exp(l2 - m)
        num = e0 * otok[0][rows, :] + e1 * otok[1][rows, :] + e2 * otok[2][rows, :]
        o_ref[rows, :] = (num / (e0 + e1 + e2)).astype(o_ref.dtype)
        return carry

    lax.fori_loop(0, seq // chunk, merge, 0)


def _t5_bucket(rel):
    half = N_BUCKETS // 2
    max_exact = half // 2
    sign = jnp.where(rel > 0, half, 0)
    n = jnp.abs(rel)
    nf = jnp.maximum(n, 1).astype(F32)
    large = max_exact + (jnp.log(nf / max_exact) / math.log(MAX_DISTANCE / max_exact)
                         * (half - max_exact)).astype(jnp.int32)
    large = jnp.minimum(large, half - 1)
    return sign + jnp.where(n < max_exact, n, large)


def _dil_bias_tiles(rel_bias):
    rel = jnp.arange(-DIL_RADIUS, DIL_RADIUS + 1)
    band = []
    for g, (window, d) in enumerate(DIL_CONFIG):
        assert window // (2 * d) == DIL_RADIUS
        band.append(rel_bias[_t5_bucket(rel * d)][:, g * DIL_HEADS:(g + 1) * DIL_HEADS].T)
    band = jnp.stack(band, axis=0).astype(F32)
    lead = band.shape[:-1]
    f = jnp.concatenate([jnp.full(lead + (DIL_QB - 1,), NEG_INF, F32), band,
                         jnp.full(lead + (DIL_QB,), NEG_INF, F32)], axis=-1)
    tile = _toeplitz(f, DIL_QB, DIL_KB, DIL_QB - 1)
    kj = np.arange(DIL_KB)
    off_seq = np.stack([((e & 1) != 0) & (kj < DIL_RADIUS) | ((e & 2) != 0) & (kj >= DIL_QB + DIL_RADIUS)
                        for e in range(4)])
    return jnp.where(off_seq[None, :, None, None, :], NEG_INF, tile[:, None])


def _dil_attention(qkv, tiles, *, batch, seq):
    max_d = max(d for _, d in DIL_CONFIG)
    assert seq % (max_d * DIL_QB) == 0
    blk = (None, seq, HEAD_DIM)

    def col(g, s):
        return lambda b, h: (b, 0, (3 * g + s) * DIL_HEADS + h)

    in_specs = [pl.BlockSpec(blk, col(g, s)) for g in range(N_GROUPS) for s in range(3)]
    in_specs.append(pl.BlockSpec((N_GROUPS, 4, None, DIL_QB, DIL_KB), lambda b, h: (0, 0, h, 0, 0)))
    pad_rows = seq + 2 * DIL_RADIUS * max_d + DIL_RADIUS
    scratch = [
        pltpu.VMEM((seq, HEAD_DIM), F32),
        pltpu.VMEM((seq, HEAD_DIM), BF16),
        pltpu.VMEM((pad_rows, HEAD_DIM), BF16),
        pltpu.VMEM((pad_rows, HEAD_DIM), BF16),
    ]
    scratch += [pltpu.VMEM((seq, HEAD_DIM), F32) for _ in range(2 * N_GROUPS)]
    return pl.pallas_call(
        functools.partial(_dil_kernel, seq=seq),
        out_shape=jax.ShapeDtypeStruct((batch, seq, DIL_HEADS * HEAD_DIM), BF16),
        grid=(batch, DIL_HEADS),
        in_specs=in_specs,
        out_specs=pl.BlockSpec(blk, lambda b, h: (b, 0, h)),
        scratch_shapes=scratch,
        compiler_params=_params("parallel", "parallel"),
        name="dil_attention",
    )(*([qkv] * 9), tiles)


def _trunk(x, p):
    batch, seq, _ = x.shape
    x = x.reshape(batch * seq, D_MODEL)
    for i in range(DEPTH):
        j = i // N_MIXERS
        x = _ffn(x, p, i, 0)
        if i % N_MIXERS == 0:
            qkv = _proj(x, p["na_qkv"], j).reshape(batch, seq, -1)
            a = _na_attention(qkv, p["na_slabs"], j, batch=batch, seq=seq)
            w_o = p["na_o"]
        else:
            qkv = _proj(x, p["dil_qkv"], j).reshape(batch, seq, -1)
            a = _dil_attention(qkv, p["dil_tiles"], batch=batch, seq=seq)
            w_o = p["dil_o"]
        x = _oproj(a.reshape(batch * seq, -1), w_o, j, x, p, i)
        x = _ffn(x, p, i, 1)
    return x.reshape(batch, seq, D_MODEL)


def kernel(x_prompt, x_sample, ln_g, ln_b, ffn_w_gate, ffn_w_up, ffn_w_down, na_w_qkv, na_w_o, na_rpb, dil_w_qkv, dil_w_o, rel_bias):
    p = {
        "ln_g": ln_g.astype(F32).reshape(DEPTH, 3, 1, D_MODEL),
        "ln_b": ln_b.astype(F32).reshape(DEPTH, 3, 1, D_MODEL),
        "wg": ffn_w_gate.astype(BF16),
        "wu": ffn_w_up.astype(BF16),
        "wd": ffn_w_down.astype(BF16),
        "na_qkv": na_w_qkv.astype(BF16),
        "na_o": na_w_o.astype(BF16),
        "na_slabs": jax.vmap(_na_bias_slabs)(na_rpb),
        "dil_qkv": dil_w_qkv.astype(BF16),
        "dil_o": dil_w_o.astype(BF16),
        "dil_tiles": _dil_bias_tiles(rel_bias),
    }
    return (_trunk(x_prompt, p), _trunk(x_sample, p))
```

```python
import functools
import math

import jax
import jax.numpy as jnp
import numpy as np
from jax import lax
from jax.experimental import pallas as pl
from jax.experimental.pallas import tpu as pltpu

D_MODEL = 2048
DEPTH = 4
GRID_W = 64
HEAD_DIM = 128
N_MIXERS = 2
NA_HEADS = D_MODEL // HEAD_DIM
NA_ROWS = 8
NA_COLS = 16
DIL_CONFIG = ((128, 1), (512, 4), (2048, 16))
N_GROUPS = len(DIL_CONFIG)
DIL_HEADS = D_MODEL // (2 * HEAD_DIM)
DIL_RADIUS = 64
N_BUCKETS = 32
MAX_DISTANCE = 1024
D_FF = 5632
ALPHA = (2 * DEPTH) ** 0.25
LN_EPS = 1e-5
NEG_INF = -1e30
SCALE = HEAD_DIM ** -0.5
LOG2E = math.log2(math.e)

VMEM_LIMIT_BYTES = 56 * 1024 * 1024

BF16 = jnp.bfloat16
F32 = jnp.float32

FFN_ROW_BLOCK = 256
OPROJ_ROW_BLOCK = 128

DIL_QB = 128
DIL_KB = DIL_QB + 2 * DIL_RADIUS

ATTN_UNROLL = 32
NA_UNROLL = 32
NA_LOOKAHEAD = 16


def _layer_norm(y, g, b, eps=LN_EPS):
    mu = jnp.mean(y, axis=-1, keepdims=True)
    yc = y - mu
    var = jnp.mean(yc * yc, axis=-1, keepdims=True)
    return yc * lax.rsqrt(var + eps) * g + b


def _params(*semantics):
    return pltpu.CompilerParams(dimension_semantics=semantics, vmem_limit_bytes=VMEM_LIMIT_BYTES)


def _ffn_kernel(x_hbm, wg_hbm, wu_hbm, wd_hbm, g_ref, b_ref, o_ref,
                x_stage, xb_ref, wg_buf, wu_buf, wd_buf, x_sem, w_sem, *, tm, tf, layer, half):
    i = pl.program_id(0)
    n_tiles = pl.num_programs(0)
    n_chunks = D_FF // tf

    def x_copy(tile):
        return pltpu.make_async_copy(x_hbm.at[pl.ds(tile * tm, tm), :], x_stage, x_sem)

    def w_copies(k, slot):
        cols = pl.ds(pl.multiple_of(k * tf, tf), tf)
        return (pltpu.make_async_copy(wg_hbm.at[layer, half, :, cols], wg_buf.at[slot], w_sem.at[0, slot]),
                pltpu.make_async_copy(wu_hbm.at[layer, half, :, cols], wu_buf.at[slot], w_sem.at[1, slot]),
                pltpu.make_async_copy(wd_hbm.at[layer, half, cols, :], wd_buf.at[slot], w_sem.at[2, slot]))

    def start_w(k, slot):
        for c in w_copies(k, slot):
            c.start()

    def wait_w(k, slot):
        for c in w_copies(k, slot):
            c.wait()

    def chunk(slot, first, last):
        row_block = FFN_ROW_BLOCK if first or last else tm
        for r in range(tm // row_block):
            rows = pl.ds(r * row_block, row_block)
            if first:
                x = x_stage[rows, :]
                xb = x.astype(BF16)
                xb_ref[rows, :] = xb
                z = (2.0 * ALPHA) * x
            else:
                xb = xb_ref[rows, :]
                z = o_ref[rows, :]
            gate = jnp.dot(xb, wg_buf[slot], preferred_element_type=F32)
            up = jnp.dot(xb, wu_buf[slot], preferred_element_type=F32)
            h = (gate * (1.0 / (1.0 + jnp.exp(-gate))) * up).astype(BF16)
            z = z + jnp.dot(h, wd_buf[slot], preferred_element_type=F32)
            if last:
                z = _layer_norm(z, g_ref[...], b_ref[...], eps=4.0 * LN_EPS)
            o_ref[rows, :] = z

    def slot_of(k):
        return lax.rem(i * n_chunks + k, 2)

    @pl.when(i == 0)
    def _prime():
        x_copy(0).start()
        start_w(0, 0)

    slot = slot_of(0)
    x_copy(i).wait()
    wait_w(0, slot)
    start_w(1, 1 - slot)
    chunk(slot, True, False)

    @pl.when(i + 1 < n_tiles)
    def _next_x():
        x_copy(i + 1).start()

    def middle(k, carry):
        slot = slot_of(k)
        wait_w(k, slot)
        start_w(k + 1, 1 - slot)
        chunk(slot, False, False)
        return carry

    lax.fori_loop(1, n_chunks - 1, middle, 0)

    slot = slot_of(n_chunks - 1)
    wait_w(n_chunks - 1, slot)

    @pl.when(i + 1 < n_tiles)
    def _next_w():
        start_w(0, 1 - slot)

    chunk(slot, False, True)


def _ln_spec(layer, idx):
    return pl.BlockSpec((None, None, 1, D_MODEL), lambda *_: (layer, idx, 0, 0))


def _ffn(x, p, layer, half, *, tm=1024, tf=512):
    m = x.shape[0]
    assert m % tm == 0 and tm % FFN_ROW_BLOCK == 0 and D_FF % tf == 0 and D_FF // tf >= 3
    return pl.pallas_call(
        functools.partial(_ffn_kernel, tm=tm, tf=tf, layer=layer, half=half),
        out_shape=jax.ShapeDtypeStruct((m, D_MODEL), F32),
        grid=(m // tm,),
        in_specs=[
            pl.BlockSpec(memory_space=pl.ANY),
            pl.BlockSpec(memory_space=pl.ANY),
            pl.BlockSpec(memory_space=pl.ANY),
            pl.BlockSpec(memory_space=pl.ANY),
            _ln_spec(layer, 2 * half),
            _ln_spec(layer, 2 * half),
        ],
        out_specs=pl.BlockSpec((tm, D_MODEL), lambda i: (i, 0)),
        scratch_shapes=[
            pltpu.VMEM((tm, D_MODEL), F32),
            pltpu.VMEM((tm, D_MODEL), BF16),
            pltpu.VMEM((2, D_MODEL, tf), BF16),
            pltpu.VMEM((2, D_MODEL, tf), BF16),
            pltpu.VMEM((2, tf, D_MODEL), BF16),
            pltpu.SemaphoreType.DMA(()),
            pltpu.SemaphoreType.DMA((3, 2)),
        ],
        compiler_params=_params("arbitrary"),
        name="ffn",
    )(x, p["wg"], p["wu"], p["wd"], p["ln_g"], p["ln_b"])


def _proj_kernel(x_ref, w_ref, o_ref, xb_ref):
    @pl.when(pl.program_id(1) == 0)
    def _init():
        xb_ref[...] = x_ref[...].astype(BF16)

    o_ref[...] = jnp.dot(xb_ref[...], w_ref[...], preferred_element_type=F32).astype(o_ref.dtype)


def _proj(x, w, idx, *, tm=1024, tn=1024):
    m, n = x.shape[0], w.shape[2]
    assert m % tm == 0 and n % tn == 0
    return pl.pallas_call(
        _proj_kernel,
        out_shape=jax.ShapeDtypeStruct((m, n), BF16),
        grid=(m // tm, n // tn),
        in_specs=[
            pl.BlockSpec((tm, D_MODEL), lambda i, j: (i, 0)),
            pl.BlockSpec((None, D_MODEL, tn), lambda i, j: (idx, 0, j)),
        ],
        out_specs=pl.BlockSpec((tm, tn), lambda i, j: (i, j)),
        scratch_shapes=[pltpu.VMEM((tm, D_MODEL), BF16)],
        compiler_params=_params("parallel", "arbitrary"),
        name="qkv_proj",
    )(x, w)


def _oproj_kernel(a_ref, w_ref, x_ref, g_ref, b_ref, o_ref):
    for r in range(a_ref.shape[0] // OPROJ_ROW_BLOCK):
        rows = pl.ds(r * OPROJ_ROW_BLOCK, OPROJ_ROW_BLOCK)
        mix = jnp.dot(a_ref[rows, :], w_ref[...], preferred_element_type=F32)
        y = ALPHA * x_ref[rows, :] + mix
        o_ref[rows, :] = _layer_norm(y, g_ref[...], b_ref[...])


def _oproj(a, w, idx, x, p, layer, *, tm=512):
    m, kdim = a.shape
    assert m % tm == 0 and tm % OPROJ_ROW_BLOCK == 0
    return pl.pallas_call(
        _oproj_kernel,
        out_shape=jax.ShapeDtypeStruct((m, D_MODEL), F32),
        grid=(m // tm,),
        in_specs=[
            pl.BlockSpec((tm, kdim), lambda i: (i, 0)),
            pl.BlockSpec((None, kdim, D_MODEL), lambda i: (idx, 0, 0)),
            pl.BlockSpec((tm, D_MODEL), lambda i: (i, 0)),
            _ln_spec(layer, 1),
            _ln_spec(layer, 1),
        ],
        out_specs=pl.BlockSpec((tm, D_MODEL), lambda i: (i, 0)),
        compiler_params=_params("parallel"),
        name="out_proj",
    )(a, w, x, p["ln_g"], p["ln_b"])


def _na_kernel(q_ref, k_ref, v_ref, bias_ref, o_ref, *, rows):
    kw = NA_ROWS * GRID_W

    def body(it, carry):
        rr = [it * NA_UNROLL + u for u in range(NA_UNROLL)]
        rs = [jnp.clip(r - NA_ROWS // 2, 0, rows - NA_ROWS) for r in rr]
        kstart = [pl.multiple_of(x * GRID_W, GRID_W) for x in rs]
        qstart = [pl.multiple_of(r * GRID_W, GRID_W) for r in rr]
        s = {}
        for step in range(NA_UNROLL + NA_LOOKAHEAD):
            if step < NA_UNROLL:
                s[step] = lax.dot_general(q_ref[pl.ds(qstart[step], GRID_W), :], k_ref[pl.ds(kstart[step], kw), :],
                                          (((1,), (1,)), ((), ())), preferred_element_type=F32)
            u = step - NA_LOOKAHEAD
            if u >= 0:
                su = s.pop(u) * (SCALE * LOG2E) + bias_ref[rs[u] - rr[u] + (NA_ROWS - 1)]
                pu = jnp.exp2(su - jnp.max(su, axis=-1, keepdims=True))
                den = jnp.sum(pu, axis=-1, keepdims=True)
                o = jnp.dot(pu.astype(BF16), v_ref[pl.ds(kstart[u], kw), :], preferred_element_type=F32)
                o_ref[pl.ds(qstart[u], GRID_W), :] = (o / den).astype(o_ref.dtype)
        return carry

    assert rows % NA_UNROLL == 0
    lax.fori_loop(0, rows // NA_UNROLL, body, 0)


def _toeplitz(f, n_rows, n_cols, offset):
    p = f.shape[-1]
    assert offset - (n_rows - 1) >= 0 and offset + n_cols <= p - 1
    lead = f.shape[:-1]
    x = jnp.broadcast_to(f[..., None, :], lead + (n_rows, p)).reshape(lead + (n_rows * p,))
    y = x[..., :n_rows * (p - 1)].reshape(lead + (n_rows, p - 1))
    return y[..., offset:offset + n_cols]


def _na_bias_slabs(rpb):
    h = rpb.shape[0]
    c = np.arange(GRID_W)
    col_start = np.clip(c - NA_COLS // 2, 0, GRID_W - NA_COLS)
    col_ok = (c[None, :] >= col_start[:, None]) & (c[None, :] < col_start[:, None] + NA_COLS)
    edge = GRID_W - NA_COLS
    f = jnp.concatenate([jnp.broadcast_to(rpb[..., :1], rpb.shape[:-1] + (edge,)), rpb,
                         jnp.broadcast_to(rpb[..., -1:], rpb.shape[:-1] + (edge + 1,))], axis=-1).astype(F32)
    table = jnp.where(col_ok, _toeplitz(f, GRID_W, GRID_W, GRID_W - 1) * LOG2E, NEG_INF)
    slabs = jnp.stack([table[:, d0:d0 + NA_ROWS] for d0 in range(NA_ROWS)], axis=1)
    return slabs.transpose(0, 1, 3, 2, 4).reshape(h, NA_ROWS, GRID_W, NA_ROWS * GRID_W)


def _na_attention(qkv, slabs, idx, *, batch, seq):
    rows = seq // GRID_W
    assert rows >= NA_ROWS and seq % GRID_W == 0
    blk = (None, seq, HEAD_DIM)
    return pl.pallas_call(
        functools.partial(_na_kernel, rows=rows),
        out_shape=jax.ShapeDtypeStruct((batch, seq, NA_HEADS * HEAD_DIM), BF16),
        grid=(batch, NA_HEADS),
        in_specs=[
            pl.BlockSpec(blk, lambda b, h: (b, 0, h)),
            pl.BlockSpec(blk, lambda b, h: (b, 0, NA_HEADS + h)),
            pl.BlockSpec(blk, lambda b, h: (b, 0, 2 * NA_HEADS + h)),
            pl.BlockSpec((None, None, NA_ROWS, GRID_W, NA_ROWS * GRID_W), lambda b, h: (idx, h, 0, 0, 0)),
        ],
        out_specs=pl.BlockSpec(blk, lambda b, h: (b, 0, h)),
        compiler_params=_params("parallel", "parallel"),
        name="na_attention",
    )(qkv, qkv, qkv, slabs)


def _dil_kernel(*refs, seq):
    qkv_refs = refs[:9]
    bias_ref, o_ref = refs[9], refs[10]
    x32, qs, kp, vp = refs[11:15]
    otok = refs[15:18]
    ltok = refs[18:21]

    for g, (_, d) in enumerate(DIL_CONFIG):
        q_ref, k_ref, v_ref = qkv_refs[3 * g:3 * g + 3]
        sub_len = seq // d
        nb = sub_len // DIL_QB
        seg = sub_len + 2 * DIL_RADIUS
        assert nb * DIL_QB == sub_len and nb & (nb - 1) == 0

        for dst in (kp, vp):
            dst[0:DIL_RADIUS, :] = jnp.zeros((DIL_RADIUS, HEAD_DIM), BF16)
            for res in range(d):
                hi = res * seg + DIL_RADIUS + sub_len
                dst[hi:hi + 2 * DIL_RADIUS, :] = jnp.zeros((2 * DIL_RADIUS, HEAD_DIM), BF16)
        if d == 1:
            qs[...] = q_ref[...]
            kp[DIL_RADIUS:DIL_RADIUS + seq, :] = k_ref[...]
            vp[DIL_RADIUS:DIL_RADIUS + seq, :] = v_ref[...]
        else:
            x32[...] = q_ref[...].astype(F32)
            for res in range(d):
                qs[res * sub_len:(res + 1) * sub_len, :] = x32[pl.ds(res, sub_len, stride=d), :].astype(BF16)
            for src, dst in ((k_ref, kp), (v_ref, vp)):
                x32[...] = src[...].astype(F32)
                for res in range(d):
                    lo = res * seg + DIL_RADIUS
                    dst[lo:lo + sub_len, :] = x32[pl.ds(res, sub_len, stride=d), :].astype(BF16)

        unroll = min(ATTN_UNROLL, seq // DIL_QB)

        def body(it, carry, g=g, d=d, nb=nb):
            nn = [it * unroll + u for u in range(unroll)]
            res = [lax.shift_right_logical(n, nb.bit_length() - 1) for n in nn]
            blk = [n - r * nb for n, r in zip(nn, res)]
            qoff = [pl.multiple_of(n * DIL_QB, DIL_QB) for n in nn]
            koff = [pl.multiple_of((n + r) * DIL_QB, DIL_QB) for n, r in zip(nn, res)]
            s = [lax.dot_general(qs[pl.ds(qoff[u], DIL_QB), :], kp[pl.ds(koff[u], DIL_KB), :],
                                 (((1,), (1,)), ((), ())), preferred_element_type=F32)
                 for u in range(unroll)]
            p, den, lse = [], [], []
            for u in range(unroll):
                edge = jnp.where(blk[u] == 0, 1, 0) + jnp.where(blk[u] == nb - 1, 2, 0)
                su = s[u] * SCALE + bias_ref[g, edge]
                m = jnp.max(su, axis=-1, keepdims=True)
                pu = jnp.exp(su - m)
                du = jnp.sum(pu, axis=-1, keepdims=True)
                p.append(pu.astype(BF16))
                den.append(du)
                lse.append(m + jnp.log(du))
            o = [jnp.dot(p[u], vp[pl.ds(koff[u], DIL_KB), :], preferred_element_type=F32)
                 for u in range(unroll)]
            for u in range(unroll):
                if d == 1:
                    rows = pl.ds(qoff[u], DIL_QB)
                else:
                    rows = pl.ds(res[u] + blk[u] * (DIL_QB * d), DIL_QB, stride=d)
                otok[g][rows, :] = o[u] / den[u]
                ltok[g][rows, :] = jnp.broadcast_to(lse[u], (DIL_QB, HEAD_DIM))
            return carry

        assert (seq // DIL_QB) % unroll == 0
        lax.fori_loop(0, seq // (DIL_QB * unroll), body, 0)

    chunk = 256

    def merge(c, carry):
        rows = pl.ds(pl.multiple_of(c * chunk, chunk), chunk)
        l0, l1, l2 = ltok[0][rows, :], ltok[1][rows, :], ltok[2][rows, :]
        m = jnp.maximum(jnp.maximum(l0, l1), l2)
        e0, e1, e2 = jnp.exp(l0 - m), jnp.exp(l1 - m), jnp.exp(l2 - m)
        num = e0 * otok[0][rows, :] + e1 * otok[1][rows, :] + e2 * otok[2][rows, :]
        o_ref[rows, :] = (num / (e0 + e1 + e2)).astype(o_ref.dtype)
        return carry

    lax.fori_loop(0, seq // chunk, merge, 0)


def _t5_bucket(rel):
    half = N_BUCKETS // 2
    max_exact = half // 2
    sign = jnp.where(rel > 0, half, 0)
    n = jnp.abs(rel)
    nf = jnp.maximum(n, 1).astype(F32)
    large = max_exact + (jnp.log(nf / max_exact) / math.log(MAX_DISTANCE / max_exact)
                         * (half - max_exact)).astype(jnp.int32)
    large = jnp.minimum(large, half - 1)
    return sign + jnp.where(n < max_exact, n, large)


def _dil_bias_tiles(rel_bias):
    rel = jnp.arange(-DIL_RADIUS, DIL_RADIUS + 1)
    band = []
    for g, (window, d) in enumerate(DIL_CONFIG):
        assert window // (2 * d) == DIL_RADIUS
        band.append(rel_bias[_t5_bucket(rel * d)][:, g * DIL_HEADS:(g + 1) * DIL_HEADS].T)
    band = jnp.stack(band, axis=0).astype(F32)
    lead = band.shape[:-1]
    f = jnp.concatenate([jnp.full(lead + (DIL_QB - 1,), NEG_INF, F32), band,
                         jnp.full(lead + (DIL_QB,), NEG_INF, F32)], axis=-1)
    tile = _toeplitz(f, DIL_QB, DIL_KB, DIL_QB - 1)
    kj = np.arange(DIL_KB)
    off_seq = np.stack([((e & 1) != 0) & (kj < DIL_RADIUS) | ((e & 2) != 0) & (kj >= DIL_QB + DIL_RADIUS)
                        for e in range(4)])
    return jnp.where(off_seq[None, :, None, None, :], NEG_INF, tile[:, None])


def _dil_attention(qkv, tiles, *, batch, seq):
    max_d = max(d for _, d in DIL_CONFIG)
    assert seq % (max_d * DIL_QB) == 0
    blk = (None, seq, HEAD_DIM)

    def col(g, s):
        return lambda b, h: (b, 0, (3 * g + s) * DIL_HEADS + h)

    in_specs = [pl.BlockSpec(blk, col(g, s)) for g in range(N_GROUPS) for s in range(3)]
    in_specs.append(pl.BlockSpec((N_GROUPS, 4, None, DIL_QB, DIL_KB), lambda b, h: (0, 0, h, 0, 0)))
    pad_rows = seq + 2 * DIL_RADIUS * max_d + DIL_RADIUS
    scratch = [
        pltpu.VMEM((seq, HEAD_DIM), F32),
        pltpu.VMEM((seq, HEAD_DIM), BF16),
        pltpu.VMEM((pad_rows, HEAD_DIM), BF16),
        pltpu.VMEM((pad_rows, HEAD_DIM), BF16),
    ]
    scratch += [pltpu.VMEM((seq, HEAD_DIM), F32) for _ in range(2 * N_GROUPS)]
    return pl.pallas_call(
        functools.partial(_dil_kernel, seq=seq),
        out_shape=jax.ShapeDtypeStruct((batch, seq, DIL_HEADS * HEAD_DIM), BF16),
        grid=(batch, DIL_HEADS),
        in_specs=in_specs,
        out_specs=pl.BlockSpec(blk, lambda b, h: (b, 0, h)),
        scratch_shapes=scratch,
        compiler_params=_params("parallel", "parallel"),
        name="dil_attention",
    )(*([qkv] * 9), tiles)


def _trunk(x, p):
    batch, seq, _ = x.shape
    x = x.reshape(batch * seq, D_MODEL)
    for i in range(DEPTH):
        j = i // N_MIXERS
        x = _ffn(x, p, i, 0)
        if i % N_MIXERS == 0:
            qkv = _proj(x, p["na_qkv"], j).reshape(batch, seq, -1)
            a = _na_attention(qkv, p["na_slabs"], j, batch=batch, seq=seq)
            w_o = p["na_o"]
        else:
            qkv = _proj(x, p["dil_qkv"], j).reshape(batch, seq, -1)
            a = _dil_attention(qkv, p["dil_tiles"], batch=batch, seq=seq)
            w_o = p["dil_o"]
        x = _oproj(a.reshape(batch * seq, -1), w_o, j, x, p, i)
        x = _ffn(x, p, i, 1)
    return x.reshape(batch, seq, D_MODEL)


def kernel(x_prompt, x_sample, ln_g, ln_b, ffn_w_gate, ffn_w_up, ffn_w_down, na_w_qkv, na_w_o, na_rpb, dil_w_qkv, dil_w_o, rel_bias):
    p = {
        "ln_g": ln_g.astype(F32).reshape(DEPTH, 3, 1, D_MODEL),
        "ln_b": ln_b.astype(F32).reshape(DEPTH, 3, 1, D_MODEL),
        "wg": ffn_w_gate.astype(BF16),
        "wu": ffn_w_up.astype(BF16),
        "wd": ffn_w_down.astype(BF16),
        "na_qkv": na_w_qkv.astype(BF16),
        "na_o": na_w_o.astype(BF16),
        "na_slabs": jax.vmap(_na_bias_slabs)(na_rpb),
        "dil_qkv": dil_w_qkv.astype(BF16),
        "dil_o": dil_w_o.astype(BF16),
        "dil_tiles": _dil_bias_tiles(rel_bias),
    }
    return (_trunk(x_prompt, p), _trunk(x_sample, p))
```

```python
import functools
import math

import jax
import jax.numpy as jnp
import numpy as np
from jax import lax
from jax.experimental import pallas as pl
from jax.experimental.pallas import tpu as pltpu

D_MODEL = 2048
DEPTH = 4
GRID_W = 64
HEAD_DIM = 128
N_MIXERS = 2
NA_HEADS = D_MODEL // HEAD_DIM
NA_ROWS = 8
NA_COLS = 16
DIL_CONFIG = ((128, 1), (512, 4), (2048, 16))
N_GROUPS = len(DIL_CONFIG)
DIL_HEADS = D_MODEL // (2 * HEAD_DIM)
DIL_RADIUS = 64
N_BUCKETS = 32
MAX_DISTANCE = 1024
D_FF = 5632
ALPHA = (2 * DEPTH) ** 0.25
LN_EPS = 1e-5
NEG_INF = -1e30
SCALE = HEAD_DIM ** -0.5
LOG2E = math.log2(math.e)

VMEM_LIMIT_BYTES = 56 * 1024 * 1024

BF16 = jnp.bfloat16
F32 = jnp.float32

FFN_ROW_BLOCK = 512
OPROJ_ROW_BLOCK = 128

DIL_QB = 128
DIL_KB = DIL_QB + 2 * DIL_RADIUS

ATTN_UNROLL = 32
NA_UNROLL = 32
NA_LOOKAHEAD = 16


def _layer_norm(y, g, b, eps=LN_EPS):
    mu = jnp.mean(y, axis=-1, keepdims=True)
    yc = y - mu
    var = jnp.mean(yc * yc, axis=-1, keepdims=True)
    return yc * lax.rsqrt(var + eps) * g + b


def _params(*semantics):
    return pltpu.CompilerParams(dimension_semantics=semantics, vmem_limit_bytes=VMEM_LIMIT_BYTES)


def _ffn_kernel(x_hbm, wg_hbm, wu_hbm, wd_hbm, g_ref, b_ref, o_ref,
                x_stage, xb_ref, wg_buf, wu_buf, wd_buf, x_sem, w_sem, *, tm, tf, layer, half):
    i = pl.program_id(0)
    n_tiles = pl.num_programs(0)
    n_chunks = D_FF // tf

    def x_copy(tile):
        return pltpu.make_async_copy(x_hbm.at[pl.ds(tile * tm, tm), :], x_stage, x_sem)

    def w_copies(k, slot):
        cols = pl.ds(pl.multiple_of(k * tf, tf), tf)
        return (pltpu.make_async_copy(wg_hbm.at[layer, half, :, cols], wg_buf.at[slot], w_sem.at[0, slot]),
                pltpu.make_async_copy(wu_hbm.at[layer, half, :, cols], wu_buf.at[slot], w_sem.at[1, slot]),
                pltpu.make_async_copy(wd_hbm.at[layer, half, cols, :], wd_buf.at[slot], w_sem.at[2, slot]))

    def start_w(k, slot):
        for c in w_copies(k, slot):
            c.start()

    def wait_w(k, slot):
        for c in w_copies(k, slot):
            c.wait()

    def chunk(slot, first, last):
        row_block = FFN_ROW_BLOCK if last else tm
        for r in range(tm // row_block):
            rows = pl.ds(r * row_block, row_block)
            if first:
                x = x_stage[rows, :]
                xb = x.astype(BF16)
                xb_ref[rows, :] = xb
                z = (2.0 * ALPHA) * x
            else:
                xb = xb_ref[rows, :]
                z = o_ref[rows, :]
            gate = jnp.dot(xb, wg_buf[slot], preferred_element_type=F32)
            up = jnp.dot(xb, wu_buf[slot], preferred_element_type=F32)
            h = (gate * (1.0 / (1.0 + jnp.exp(-gate))) * up).astype(BF16)
            z = z + jnp.dot(h, wd_buf[slot], preferred_element_type=F32)
            if last:
                z = _layer_norm(z, g_ref[...], b_ref[...], eps=4.0 * LN_EPS)
            o_ref[rows, :] = z

    def slot_of(k):
        return lax.rem(i * n_chunks + k, 2)

    @pl.when(i == 0)
    def _prime():
        x_copy(0).start()
        start_w(0, 0)

    slot = slot_of(0)
    x_copy(i).wait()
    wait_w(0, slot)
    start_w(1, 1 - slot)
    chunk(slot, True, False)

    @pl.when(i + 1 < n_tiles)
    def _next_x():
        x_copy(i + 1).start()

    def middle(k, carry):
        slot = slot_of(k)
        wait_w(k, slot)
        start_w(k + 1, 1 - slot)
        chunk(slot, False, False)
        return carry

    lax.fori_loop(1, n_chunks - 1, middle, 0)

    slot = slot_of(n_chunks - 1)
    wait_w(n_chunks - 1, slot)

    @pl.when(i + 1 < n_tiles)
    def _next_w():
        start_w(0, 1 - slot)

    chunk(slot, False, True)


def _ln_spec(layer, idx):
    return pl.BlockSpec((None, None, 1, D_MODEL), lambda *_: (layer, idx, 0, 0))


def _ffn(x, p, layer, half, *, tm=1024, tf=512):
    m = x.shape[0]
    assert m % tm == 0 and tm % FFN_ROW_BLOCK == 0 and D_FF % tf == 0 and D_FF // tf >= 3
    return pl.pallas_call(
        functools.partial(_ffn_kernel, tm=tm, tf=tf, layer=layer, half=half),
        out_shape=jax.ShapeDtypeStruct((m, D_MODEL), F32),
        grid=(m // tm,),
        in_specs=[
            pl.BlockSpec(memory_space=pl.ANY),
            pl.BlockSpec(memory_space=pl.ANY),
            pl.BlockSpec(memory_space=pl.ANY),
            pl.BlockSpec(memory_space=pl.ANY),
            _ln_spec(layer, 2 * half),
            _ln_spec(layer, 2 * half),
        ],
        out_specs=pl.BlockSpec((tm, D_MODEL), lambda i: (i, 0)),
        scratch_shapes=[
            pltpu.VMEM((tm, D_MODEL), F32),
            pltpu.VMEM((tm, D_MODEL), BF16),
            pltpu.VMEM((2, D_MODEL, tf), BF16),
            pltpu.VMEM((2, D_MODEL, tf), BF16),
            pltpu.VMEM((2, tf, D_MODEL), BF16),
            pltpu.SemaphoreType.DMA(()),
            pltpu.SemaphoreType.DMA((3, 2)),
        ],
        compiler_params=_params("arbitrary"),
        name="ffn",
    )(x, p["wg"], p["wu"], p["wd"], p["ln_g"], p["ln_b"])


def _proj_kernel(x_ref, w_ref, o_ref, xb_ref):
    @pl.when(pl.program_id(1) == 0)
    def _init():
        xb_ref[...] = x_ref[...].astype(BF16)

    o_ref[...] = jnp.dot(xb_ref[...], w_ref[...], preferred_element_type=F32).astype(o_ref.dtype)


def _proj(x, w, idx, *, tm=1024, tn=1536):
    m, n = x.shape[0], w.shape[2]
    assert m % tm == 0 and n % tn == 0
    return pl.pallas_call(
        _proj_kernel,
        out_shape=jax.ShapeDtypeStruct((m, n), BF16),
        grid=(m // tm, n // tn),
        in_specs=[
            pl.BlockSpec((tm, D_MODEL), lambda i, j: (i, 0)),
            pl.BlockSpec((None, D_MODEL, tn), lambda i, j: (idx, 0, j)),
        ],
        out_specs=pl.BlockSpec((tm, tn), lambda i, j: (i, j)),
        scratch_shapes=[pltpu.VMEM((tm, D_MODEL), BF16)],
        compiler_params=_params("parallel", "arbitrary"),
        name="qkv_proj",
    )(x, w)


def _oproj_kernel(a_ref, w_ref, x_ref, g_ref, b_ref, o_ref):
    for r in range(a_ref.shape[0] // OPROJ_ROW_BLOCK):
        rows = pl.ds(r * OPROJ_ROW_BLOCK, OPROJ_ROW_BLOCK)
        mix = jnp.dot(a_ref[rows, :], w_ref[...], preferred_element_type=F32)
        y = ALPHA * x_ref[rows, :] + mix
        o_ref[rows, :] = _layer_norm(y, g_ref[...], b_ref[...])


def _oproj(a, w, idx, x, p, layer, *, tm=512):
    m, kdim = a.shape
    assert m % tm == 0 and tm % OPROJ_ROW_BLOCK == 0
    return pl.pallas_call(
        _oproj_kernel,
        out_shape=jax.ShapeDtypeStruct((m, D_MODEL), F32),
        grid=(m // tm,),
        in_specs=[
            pl.BlockSpec((tm, kdim), lambda i: (i, 0)),
            pl.BlockSpec((None, kdim, D_MODEL), lambda i: (idx, 0, 0)),
            pl.BlockSpec((tm, D_MODEL), lambda i: (i, 0)),
            _ln_spec(layer, 1),
            _ln_spec(layer, 1),
        ],
        out_specs=pl.BlockSpec((tm, D_MODEL), lambda i: (i, 0)),
        compiler_params=_params("parallel"),
        name="out_proj",
    )(a, w, x, p["ln_g"], p["ln_b"])


def _na_kernel(q_ref, k_ref, v_ref, bias_ref, o_ref, *, rows):
    kw = NA_ROWS * GRID_W

    def body(it, carry):
        rr = [it * NA_UNROLL + u for u in range(NA_UNROLL)]
        rs = [jnp.clip(r - NA_ROWS // 2, 0, rows - NA_ROWS) for r in rr]
        kstart = [pl.multiple_of(x * GRID_W, GRID_W) for x in rs]
        qstart = [pl.multiple_of(r * GRID_W, GRID_W) for r in rr]
        s = {}
        for step in range(NA_UNROLL + NA_LOOKAHEAD):
            if step < NA_UNROLL:
                s[step] = lax.dot_general(q_ref[pl.ds(qstart[step], GRID_W), :], k_ref[pl.ds(kstart[step], kw), :],
                                          (((1,), (1,)), ((), ())), preferred_element_type=F32)
            u = step - NA_LOOKAHEAD
            if u >= 0:
                su = s.pop(u) * (SCALE * LOG2E) + bias_ref[rs[u] - rr[u] + (NA_ROWS - 1)]
                pu = jnp.exp2(su - jnp.max(su, axis=-1, keepdims=True))
                den = jnp.sum(pu, axis=-1, keepdims=True)
                o = jnp.dot(pu.astype(BF16), v_ref[pl.ds(kstart[u], kw), :], preferred_element_type=F32)
                o_ref[pl.ds(qstart[u], GRID_W), :] = (o / den).astype(o_ref.dtype)
        return carry

    assert rows % NA_UNROLL == 0
    lax.fori_loop(0, rows // NA_UNROLL, body, 0)


def _toeplitz(f, n_rows, n_cols, offset):
    p = f.shape[-1]
    assert offset - (n_rows - 1) >= 0 and offset + n_cols <= p - 1
    lead = f.shape[:-1]
    x = jnp.broadcast_to(f[..., None, :], lead + (n_rows, p)).reshape(lead + (n_rows * p,))
    y = x[..., :n_rows * (p - 1)].reshape(lead + (n_rows, p - 1))
    return y[..., offset:offset + n_cols]


def _na_bias_slabs(rpb):
    h = rpb.shape[0]
    c = np.arange(GRID_W)
    col_start = np.clip(c - NA_COLS // 2, 0, GRID_W - NA_COLS)
    col_ok = (c[None, :] >= col_start[:, None]) & (c[None, :] < col_start[:, None] + NA_COLS)
    edge = GRID_W - NA_COLS
    f = jnp.concatenate([jnp.broadcast_to(rpb[..., :1], rpb.shape[:-1] + (edge,)), rpb,
                         jnp.broadcast_to(rpb[..., -1:], rpb.shape[:-1] + (edge + 1,))], axis=-1).astype(F32)
    table = jnp.where(col_ok, _toeplitz(f, GRID_W, GRID_W, GRID_W - 1) * LOG2E, NEG_INF)
    slabs = jnp.stack([table[:, d0:d0 + NA_ROWS] for d0 in range(NA_ROWS)], axis=1)
    return slabs.transpose(0, 1, 3, 2, 4).reshape(h, NA_ROWS, GRID_W, NA_ROWS * GRID_W)


def _na_attention(qkv, slabs, idx, *, batch, seq):
    rows = seq // GRID_W
    assert rows >= NA_ROWS and seq % GRID_W == 0
    blk = (None, seq, HEAD_DIM)
    return pl.pallas_call(
        functools.partial(_na_kernel, rows=rows),
        out_shape=jax.ShapeDtypeStruct((batch, seq, NA_HEADS * HEAD_DIM), BF16),
        grid=(batch, NA_HEADS),
        in_specs=[
            pl.BlockSpec(blk, lambda b, h: (b, 0, h)),
            pl.BlockSpec(blk, lambda b, h: (b, 0, NA_HEADS + h)),
            pl.BlockSpec(blk, lambda b, h: (b, 0, 2 * NA_HEADS + h)),
            pl.BlockSpec((None, None, NA_ROWS, GRID_W, NA_ROWS * GRID_W), lambda b, h: (idx, h, 0, 0, 0)),
        ],
        out_specs=pl.BlockSpec(blk, lambda b, h: (b, 0, h)),
        compiler_params=_params("parallel", "parallel"),
        name="na_attention",
    )(qkv, qkv, qkv, slabs)


def _dil_kernel(*refs, seq):
    qkv_refs = refs[:9]
    bias_ref, o_ref = refs[9], refs[10]
    x32, qs, kp, vp = refs[11:15]
    otok = refs[15:18]
    ltok = refs[18:21]

    for g, (_, d) in enumerate(DIL_CONFIG):
        q_ref, k_ref, v_ref = qkv_refs[3 * g:3 * g + 3]
        sub_len = seq // d
        nb = sub_len // DIL_QB
        seg = sub_len + 2 * DIL_RADIUS
        assert nb * DIL_QB == sub_len and nb & (nb - 1) == 0

        for dst in (kp, vp):
            dst[0:DIL_RADIUS, :] = jnp.zeros((DIL_RADIUS, HEAD_DIM), BF16)
            for res in range(d):
                hi = res * seg + DIL_RADIUS + sub_len
                dst[hi:hi + 2 * DIL_RADIUS, :] = jnp.zeros((2 * DIL_RADIUS, HEAD_DIM), BF16)
        if d == 1:
            qs[...] = q_ref[...]
            kp[DIL_RADIUS:DIL_RADIUS + seq, :] = k_ref[...]
            vp[DIL_RADIUS:DIL_RADIUS + seq, :] = v_ref[...]
        else:
            x32[...] = q_ref[...].astype(F32)
            for res in range(d):
                qs[res * sub_len:(res + 1) * sub_len, :] = x32[pl.ds(res, sub_len, stride=d), :].astype(BF16)
            for src, dst in ((k_ref, kp), (v_ref, vp)):
                x32[...] = src[...].astype(F32)
                for res in range(d):
                    lo = res * seg + DIL_RADIUS
                    dst[lo:lo + sub_len, :] = x32[pl.ds(res, sub_len, stride=d), :].astype(BF16)

        unroll = min(ATTN_UNROLL, seq // DIL_QB)

        def body(it, carry, g=g, d=d, nb=nb):
            nn = [it * unroll + u for u in range(unroll)]
            res = [lax.shift_right_logical(n, nb.bit_length() - 1) for n in nn]
            blk = [n - r * nb for n, r in zip(nn, res)]
            qoff = [pl.multiple_of(n * DIL_QB, DIL_QB) for n in nn]
            koff = [pl.multiple_of((n + r) * DIL_QB, DIL_QB) for n, r in zip(nn, res)]
            s = [lax.dot_general(qs[pl.ds(qoff[u], DIL_QB), :], kp[pl.ds(koff[u], DIL_KB), :],
                                 (((1,), (1,)), ((), ())), preferred_element_type=F32)
                 for u in range(unroll)]
            p, den, lse = [], [], []
            for u in range(unroll):
                edge = jnp.where(blk[u] == 0, 1, 0) + jnp.where(blk[u] == nb - 1, 2, 0)
                su = s[u] * SCALE + bias_ref[g, edge]
                m = jnp.max(su, axis=-1, keepdims=True)
                pu = jnp.exp(su - m)
                du = jnp.sum(pu, axis=-1, keepdims=True)
                p.append(pu.astype(BF16))
                den.append(du)
                lse.append(m + jnp.log(du))
            o = [jnp.dot(p[u], vp[pl.ds(koff[u], DIL_KB), :], preferred_element_type=F32)
                 for u in range(unroll)]
            for u in range(unroll):
                if d == 1:
                    rows = pl.ds(qoff[u], DIL_QB)
                else:
                    rows = pl.ds(res[u] + blk[u] * (DIL_QB * d), DIL_QB, stride=d)
                otok[g][rows, :] = o[u] / den[u]
                ltok[g][rows, :] = jnp.broadcast_to(lse[u], (DIL_QB, HEAD_DIM))
            return carry

        assert (seq // DIL_QB) % unroll == 0
        lax.fori_loop(0, seq // (DIL_QB * unroll), body, 0)

    chunk = 256

    def merge(c, carry):
        rows = pl.ds(pl.multiple_of(c * chunk, chunk), chunk)
        l0, l1, l2 = ltok[0][rows, :], ltok[1][rows, :], ltok[2][rows, :]
        m = jnp.maximum(jnp.maximum(l0, l1), l2)
        e0, e1, e2 = jnp.exp(l0 - m), jnp.exp(l1 - m), jnp.exp(l2 - m)
        num = e0 * otok[0][rows, :] + e1 * otok[1][rows, :] + e2 * otok[2][rows, :]
        o_ref[rows, :] = (num / (e0 + e1 + e2)).astype(o_ref.dtype)
        return carry

    lax.fori_loop(0, seq // chunk, merge, 0)


def _t5_bucket(rel):
    half = N_BUCKETS // 2
    max_exact = half // 2
    sign = jnp.where(rel > 0, half, 0)
    n = jnp.abs(rel)
    nf = jnp.maximum(n, 1).astype(F32)
    large = max_exact + (jnp.log(nf / max_exact) / math.log(MAX_DISTANCE / max_exact)
                         * (half - max_exact)).astype(jnp.int32)
    large = jnp.minimum(large, half - 1)
    return sign + jnp.where(n < max_exact, n, large)


def _dil_bias_tiles(rel_bias):
    rel = jnp.arange(-DIL_RADIUS, DIL_RADIUS + 1)
    band = []
    for g, (window, d) in enumerate(DIL_CONFIG):
        assert window // (2 * d) == DIL_RADIUS
        band.append(rel_bias[_t5_bucket(rel * d)][:, g * DIL_HEADS:(g + 1) * DIL_HEADS].T)
    band = jnp.stack(band, axis=0).astype(F32)
    lead = band.shape[:-1]
    f = jnp.concatenate([jnp.full(lead + (DIL_QB - 1,), NEG_INF, F32), band,
                         jnp.full(lead + (DIL_QB,), NEG_INF, F32)], axis=-1)
    tile = _toeplitz(f, DIL_QB, DIL_KB, DIL_QB - 1)
    kj = np.arange(DIL_KB)
    off_seq = np.stack([((e & 1) != 0) & (kj < DIL_RADIUS) | ((e & 2) != 0) & (kj >= DIL_QB + DIL_RADIUS)
                        for e in range(4)])
    return jnp.where(off_seq[None, :, None, None, :], NEG_INF, tile[:, None])


def _dil_attention(qkv, tiles, *, batch, seq):
    max_d = max(d for _, d in DIL_CONFIG)
    assert seq % (max_d * DIL_QB) == 0
    blk = (None, seq, HEAD_DIM)

    def col(g, s):
        return lambda b, h: (b, 0, (3 * g + s) * DIL_HEADS + h)

    in_specs = [pl.BlockSpec(blk, col(g, s)) for g in range(N_GROUPS) for s in range(3)]
    in_specs.append(pl.BlockSpec((N_GROUPS, 4, None, DIL_QB, DIL_KB), lambda b, h: (0, 0, h, 0, 0)))
    pad_rows = seq + 2 * DIL_RADIUS * max_d + DIL_RADIUS
    scratch = [
        pltpu.VMEM((seq, HEAD_DIM), F32),
        pltpu.VMEM((seq, HEAD_DIM), BF16),
        pltpu.VMEM((pad_rows, HEAD_DIM), BF16),
        pltpu.VMEM((pad_rows, HEAD_DIM), BF16),
    ]
    scratch += [pltpu.VMEM((seq, HEAD_DIM), F32) for _ in range(2 * N_GROUPS)]
    return pl.pallas_call(
        functools.partial(_dil_kernel, seq=seq),
        out_shape=jax.ShapeDtypeStruct((batch, seq, DIL_HEADS * HEAD_DIM), BF16),
        grid=(batch, DIL_HEADS),
        in_specs=in_specs,
        out_specs=pl.BlockSpec(blk, lambda b, h: (b, 0, h)),
        scratch_shapes=scratch,
        compiler_params=_params("parallel", "parallel"),
        name="dil_attention",
    )(*([qkv] * 9), tiles)


def _trunk(x, p):
    batch, seq, _ = x.shape
    x = x.reshape(batch * seq, D_MODEL)
    for i in range(DEPTH):
        j = i // N_MIXERS
        x = _ffn(x, p, i, 0)
        if i % N_MIXERS == 0:
            qkv = _proj(x, p["na_qkv"], j).reshape(batch, seq, -1)
            a = _na_attention(qkv, p["na_slabs"], j, batch=batch, seq=seq)
            w_o = p["na_o"]
        else:
            qkv = _proj(x, p["dil_qkv"], j).reshape(batch, seq, -1)
            a = _dil_attention(qkv, p["dil_tiles"], batch=batch, seq=seq)
            w_o = p["dil_o"]
        x = _oproj(a.reshape(batch * seq, -1), w_o, j, x, p, i)
        x = _ffn(x, p, i, 1)
    return x.reshape(batch, seq, D_MODEL)


def kernel(x_prompt, x_sample, ln_g, ln_b, ffn_w_gate, ffn_w_up, ffn_w_down, na_w_qkv, na_w_o, na_rpb, dil_w_qkv, dil_w_o, rel_bias):
    p = {
        "ln_g": ln_g.astype(F32).reshape(DEPTH, 3, 1, D_MODEL),
        "ln_b": ln_b.astype(F32).reshape(DEPTH, 3, 1, D_MODEL),
        "wg": ffn_w_gate.astype(BF16),
        "wu": ffn_w_up.astype(BF16),
        "wd": ffn_w_down.astype(BF16),
        "na_qkv": na_w_qkv.astype(BF16),
        "na_o": na_w_o.astype(BF16),
        "na_slabs": jax.vmap(_na_bias_slabs)(na_rpb),
        "dil_qkv": dil_w_qkv.astype(BF16),
        "dil_o": dil_w_o.astype(BF16),
        "dil_tiles": _dil_bias_tiles(rel_bias),
    }
    return (_trunk(x_prompt, p), _trunk(x_sample, p))
```
